```python
import functools
import jax
import jax.numpy as jnp
from jax import lax
import numpy as np

D_MODEL = 1024
BATCH = 8
SEQ = 8192
DEPTH = 1
DEC_BATCH = 16
DEC_SEQ = 32
PAST_LEN = 1024

CHUNK = 64
HEAD_DIM = 64
N_HEADS_R = 8
D_R = N_HEADS_R * HEAD_DIM
LORA_W = 64
LORA_A = 64
LORA_G = 128
R_PROJ = 3 * D_R + LORA_W + LORA_A + LORA_G
N_HEADS_A = 8
D_A = N_HEADS_A * HEAD_DIM
BAND_CHUNKS = 8
WINDOW = BAND_CHUNKS * CHUNK
BAND = WINDOW + CHUNK
REL_CLIP = 128
N_REL = 2 * REL_CLIP + 1
D_FF = 4 * D_MODEL
IN_PROJ = R_PROJ + 3 * D_A + 2 * D_MODEL
NORM_EPS = 1e-6
GN_EPS = 64e-5
ATT_SCALE = HEAD_DIM ** -0.5

kernel_name = "hybrid_rwkv7_chunkband_stream_step"


def rms_norm(x, g):
    xf = x.astype(jnp.float32)
    y = xf * lax.rsqrt(jnp.mean(xf * xf, axis=-1, keepdims=True) + NORM_EPS)
    return (y * g.astype(jnp.float32)).astype(x.dtype)


def modulate(x, g, shift, scale):
    return rms_norm(x, g) * (1 + scale) + shift


def wkv7_scan(s0, r, w, k, v, kk, a):
    def step(S, inp):
        r_t, w_t, k_t, v_t, kk_t, a_t = inp
        sa = jnp.einsum("bhvk,bhk->bhv", S, kk_t)
        S = (S * w_t[:, :, None, :]
             - sa[..., None] * (kk_t * a_t)[:, :, None, :]
             + v_t[..., None] * k_t[:, :, None, :])
        return S, jnp.einsum("bhvk,bhk->bhv", S, r_t)
    xs = tuple(jnp.swapaxes(t.astype(jnp.float32), 0, 1) for t in (r, w, k, v, kk, a))
    S, o = lax.scan(step, s0.astype(jnp.float32), xs)
    return S, jnp.swapaxes(o, 0, 1)


def rwkv7_branch(p, prev_row, s0, mu, w_up, w0, a_up, a0, g_up, k_k, k_a, r_k, lnx_g, lnx_b):
    B, T, _ = p.shape
    p_prev = jnp.concatenate([prev_row[:, None, :].astype(p.dtype), p[:, :-1]], axis=1)
    xs = p + (p_prev - p) * mu
    r, k, v, xw, xa, xg = jnp.split(
        xs, [D_R, 2 * D_R, 3 * D_R, 3 * D_R + LORA_W, 3 * D_R + LORA_W + LORA_A], axis=-1)
    w_log = -jax.nn.softplus(-(w0 + jnp.tanh(xw) @ w_up)) - 0.5
    decay = jnp.exp(-jnp.exp(w_log.astype(jnp.float32)))
    a = jax.nn.sigmoid(a0 + xa @ a_up)
    g = jax.nn.sigmoid(xg) @ g_up
    heads = lambda t: t.reshape(B, T, N_HEADS_R, HEAD_DIM)
    kk = heads(k * k_k).astype(jnp.float32)
    kk = kk * lax.rsqrt(jnp.maximum(jnp.sum(kk * kk, axis=-1, keepdims=True), 1e-24))
    k = k * (1 + (a - 1) * k_a)
    rh, kh, vh = heads(r), heads(k), heads(v)
    S, o = wkv7_scan(s0, rh, heads(decay), kh, vh, kk, heads(a))
    mean = jnp.mean(o, axis=-1, keepdims=True)
    var = jnp.mean(jnp.square(o - mean), axis=-1, keepdims=True)
    o = ((o - mean) * lax.rsqrt(var + GN_EPS)).reshape(B, T, D_R)
    o = (o * lnx_g.astype(jnp.float32) + lnx_b.astype(jnp.float32)).astype(p.dtype)
    bonus = (jnp.sum(rh * kh * r_k, axis=-1, keepdims=True) * vh).reshape(B, T, D_R)
    return (o + bonus) * g, S.astype(s0.dtype), p[:, -1]


def rel_bias(table, q_pos, k_pos):
    idx = jnp.clip(q_pos[:, None] - k_pos[None, :], -REL_CLIP, REL_CLIP) + REL_CLIP
    return table[:, idx].astype(jnp.float32)


def band_attend(q, k, v, bias, valid):
    s = jnp.einsum("bqhd,bkhd->bhqk", q, k).astype(jnp.float32) * ATT_SCALE + bias
    s = jnp.where(valid, s, jnp.finfo(jnp.float32).min)
    p = jax.nn.softmax(s, axis=-1).astype(v.dtype)
    return jnp.einsum("bhqk,bkhd->bqhd", p, v)


def chunk_band_attention_prompt(q, k, v, table):
    B, T, H, Dh = q.shape
    n_chunks = T // CHUNK
    kp = jnp.pad(k, ((0, 0), (WINDOW, 0), (0, 0), (0, 0)))
    vp = jnp.pad(v, ((0, 0), (WINDOW, 0), (0, 0), (0, 0)))
    k_off = jnp.arange(BAND) - WINDOW
    bias = rel_bias(table, jnp.arange(CHUNK), k_off)

    def one_chunk(ci):
        start = ci * CHUNK
        qc = lax.dynamic_slice_in_dim(q, start, CHUNK, axis=1)
        kc = lax.dynamic_slice_in_dim(kp, start, BAND, axis=1)
        vc = lax.dynamic_slice_in_dim(vp, start, BAND, axis=1)
        return band_attend(qc, kc, vc, bias, (start + k_off) >= 0)

    out = lax.map(one_chunk, jnp.arange(n_chunks))
    return jnp.moveaxis(out, 0, 1).reshape(B, T, H, Dh)


def chunk_band_attention_step(q, k, v, table, cache_k, cache_v):
    T = q.shape[1]
    L = cache_k.shape[1]
    k_all = jnp.concatenate([cache_k.astype(k.dtype), k], axis=1)
    v_all = jnp.concatenate([cache_v.astype(v.dtype), v], axis=1)
    q_pos = PAST_LEN + jnp.arange(T)
    k_pos = jnp.concatenate([PAST_LEN - L + jnp.arange(L), q_pos])
    return band_attend(q, k_all, v_all, rel_bias(table, q_pos, k_pos), k_pos >= 0)


def trunk_layer(x, c, rwkv_prev, rwkv_s0, attend, P, l):
    B, T, _ = x.shape
    mod = (jax.nn.silu(c) @ P["w_ada"][l] + P["b_ada"][l])[:, None, :]
    sh1, sc1, g1, sh2, sc2, g2 = jnp.split(mod, 6, axis=-1)
    h = modulate(x, P["norm_mix_g"][l], sh1, sc1)
    proj = h @ P["w_in"][l]
    p_r, q, k, v, gates = jnp.split(
        proj, [R_PROJ, R_PROJ + D_A, R_PROJ + 2 * D_A, R_PROJ + 3 * D_A], axis=-1)
    y_r, s_new, last_row = rwkv7_branch(
        p_r, rwkv_prev, rwkv_s0, P["mu_shift"][l], P["w_lora_up"][l], P["w0"][l],
        P["a_lora_up"][l], P["a0"][l], P["g_lora_up"][l], P["k_k"][l], P["k_a"][l],
        P["r_k"][l], P["lnx_g"][l], P["lnx_b"][l])
    q, k, v = (t.reshape(B, T, N_HEADS_A, HEAD_DIM) for t in (q, k, v))
    y_a = attend(q, k, v, P["rel_table"][l]).reshape(B, T, D_A)
    gate_r, gate_a = jnp.split(jax.nn.sigmoid(gates), 2, axis=-1)
    merged = gate_r * (y_r @ P["w_br_r"][l]) + gate_a * (y_a @ P["w_br_a"][l])
    x = x + g1 * (merged @ P["w_out"][l])
    h2 = modulate(x, P["norm_mlp_g"][l], sh2, sc2)
    x = x + g2 * (jnp.square(jax.nn.relu(h2 @ P["w_ff1"][l])) @ P["w_ff2"][l])
    return x, s_new, last_row, k, v


def setup_inputs(seed: int = 0) -> dict:
    key = jax.random.key(seed)
    ks = iter(jax.random.split(key, 40))
    f32 = jnp.float32

    def nrm(shape, scale):
        return jax.random.normal(next(ks), shape, f32) * scale

    n_cache = min(WINDOW, PAST_LEN)
    D = D_MODEL
    return {
        "x_prompt": nrm((BATCH, SEQ, D), 1.0),
        "x_sample": nrm((DEC_BATCH, DEC_SEQ, D), 1.0),
        "c_prompt": nrm((BATCH, D), 1.0),
        "c_sample": nrm((DEC_BATCH, D), 1.0),
        "state_rwkv_wkv": nrm((DEPTH, DEC_BATCH, N_HEADS_R, HEAD_DIM, HEAD_DIM), 0.3),
        "state_rwkv_shift": nrm((DEPTH, DEC_BATCH, R_PROJ), 1.0),
        "cache_att_k": nrm((DEPTH, DEC_BATCH, n_cache, N_HEADS_A, HEAD_DIM), 1.0),
        "cache_att_v": nrm((DEPTH, DEC_BATCH, n_cache, N_HEADS_A, HEAD_DIM), 1.0),
        "w_ada": nrm((DEPTH, D, 6 * D), 0.5 * D ** -0.5),
        "b_ada": nrm((DEPTH, 6 * D), 0.02),
        "norm_mix_g": 1.0 + nrm((DEPTH, D), 0.05),
        "w_in": nrm((DEPTH, D, IN_PROJ), D ** -0.5),
        "mu_shift": jax.random.uniform(next(ks), (DEPTH, R_PROJ), f32),
        "w_lora_up": nrm((DEPTH, LORA_W, D_R), LORA_W ** -0.5),
        "w0": nrm((DEPTH, D_R), 0.5),
        "a_lora_up": nrm((DEPTH, LORA_A, D_R), 0.5 * LORA_A ** -0.5),
        "a0": nrm((DEPTH, D_R), 0.1),
        "g_lora_up": nrm((DEPTH, LORA_G, D_R), LORA_G ** -0.5),
        "k_k": 0.85 + nrm((DEPTH, D_R), 0.05),
        "k_a": 1.0 + nrm((DEPTH, D_R), 0.05),
        "r_k": nrm((DEPTH, N_HEADS_R, HEAD_DIM), 0.1),
        "lnx_g": 1.0 + nrm((DEPTH, D_R), 0.05),
        "lnx_b": nrm((DEPTH, D_R), 0.02),
        "rel_table": nrm((DEPTH, N_HEADS_A, N_REL), 0.1),
        "w_br_r": nrm((DEPTH, D_R, D), D_R ** -0.5),
        "w_br_a": nrm((DEPTH, D_A, D), D_A ** -0.5),
        "w_out": nrm((DEPTH, D, D), D ** -0.5),
        "norm_mlp_g": 1.0 + nrm((DEPTH, D), 0.05),
        "w_ff1": nrm((DEPTH, D, D_FF), D ** -0.5),
        "w_ff2": nrm((DEPTH, D_FF, D), D_FF ** -0.5),
        "final_norm_g": 1.0 + nrm((D,), 0.05),
    }


def reference(x_prompt, x_sample, c_prompt, c_sample, state_rwkv_wkv, state_rwkv_shift,
              cache_att_k, cache_att_v, w_ada, b_ada, norm_mix_g, w_in, mu_shift, w_lora_up, w0,
              a_lora_up, a0, g_lora_up, k_k, k_a, r_k, lnx_g, lnx_b, rel_table, w_br_r, w_br_a,
              w_out, norm_mlp_g, w_ff1, w_ff2, final_norm_g):
    P = dict(w_ada=w_ada, b_ada=b_ada, norm_mix_g=norm_mix_g, w_in=w_in, mu_shift=mu_shift,
             w_lora_up=w_lora_up, w0=w0, a_lora_up=a_lora_up, a0=a0, g_lora_up=g_lora_up,
             k_k=k_k, k_a=k_a, r_k=r_k, lnx_g=lnx_g, lnx_b=lnx_b, rel_table=rel_table,
             w_br_r=w_br_r, w_br_a=w_br_a, w_out=w_out, norm_mlp_g=norm_mlp_g,
             w_ff1=w_ff1, w_ff2=w_ff2)
    Bp, Tp, _ = x_prompt.shape
    n_keep = min(WINDOW, Tp)
    xp, xs = x_prompt, x_sample
    p_wkv, p_shift, p_k, p_v = [], [], [], []
    s_wkv, s_shift, s_k, s_v = [], [], [], []
    for l in range(DEPTH):
        zero_row = jnp.zeros((Bp, R_PROJ), x_prompt.dtype)
        zero_state = jnp.zeros((Bp, N_HEADS_R, HEAD_DIM, HEAD_DIM), x_prompt.dtype)
        xp, sp, rp, kp, vp = trunk_layer(xp, c_prompt, zero_row, zero_state,
                                         chunk_band_attention_prompt, P, l)
        p_wkv.append(sp)
        p_shift.append(rp)
        p_k.append(kp[:, Tp - n_keep:])
        p_v.append(vp[:, Tp - n_keep:])
        attend = functools.partial(chunk_band_attention_step,
                                   cache_k=cache_att_k[l], cache_v=cache_att_v[l])
        xs, ss, rs, ks_, vs_ = trunk_layer(xs, c_sample, state_rwkv_shift[l], state_rwkv_wkv[l],
                                           attend, P, l)
        s_wkv.append(ss)
        s_shift.append(rs)
        s_k.append(ks_)
        s_v.append(vs_)
    y_prompt = rms_norm(xp, final_norm_g)
    y_sample = rms_norm(xs, final_norm_g)
    return (y_prompt, y_sample,
            jnp.stack(p_wkv), jnp.stack(p_shift), jnp.stack(p_k), jnp.stack(p_v),
            jnp.stack(s_wkv), jnp.stack(s_shift), jnp.stack(s_k), jnp.stack(s_v))
```

```python
import functools
import math

import jax
import jax.numpy as jnp
from jax import lax
from jax.experimental import pallas as pl
from jax.experimental.pallas import tpu as pltpu

D_MODEL = 1024
HEAD_DIM = 64
N_HEADS = 8
D_R = N_HEADS * HEAD_DIM
D_A = N_HEADS * HEAD_DIM
LORA_W = 64
LORA_A = 64
LORA_G = 128
R_PROJ = 3 * D_R + LORA_W + LORA_A + LORA_G
CHUNK = 64
BAND_CHUNKS = 8
WINDOW = BAND_CHUNKS * CHUNK
BAND = WINDOW + CHUNK
REL_CLIP = 128
D_FF = 4 * D_MODEL
IN_PROJ = R_PROJ + 3 * D_A + 2 * D_MODEL
NORM_EPS = 1e-6
GN_EPS = 64e-5
ATT_SCALE = HEAD_DIM ** -0.5
PAST_LEN = 1024

LANES = 128
N_PAIRS = N_HEADS // 2
NEG_BIG = -1e30
VMEM_LIMIT = 56 * 1024 * 1024

F32 = jnp.float32
BF16 = jnp.bfloat16


def _dot(a, b):
    return lax.dot_general(a, b, (((1,), (0,)), ((), ())), preferred_element_type=F32)


def _dot_nt(a, b):
    return lax.dot_general(a, b, (((1,), (1,)), ((), ())), preferred_element_type=F32)


def _dot_tn(a, b):
    return lax.dot_general(a, b, (((0,), (0,)), ((), ())), preferred_element_type=F32)


def _split_dot(x, w, terms):
    acc = None
    rem = x
    for _ in range(terms):
        piece = rem.astype(BF16)
        part = _dot(piece, w)
        acc = part if acc is None else acc + part
        rem = rem - piece.astype(F32)
    return acc


def _rms(x):
    return x * lax.rsqrt(jnp.mean(x * x, axis=-1, keepdims=True) + NORM_EPS)


def _const_spec(shape):
    nd = len(shape)
    return pl.BlockSpec(shape, lambda *_: (0,) * nd, pipeline_mode=pl.Buffered(1))


def _ada_kernel(c_ref, w_ref, b_ref, o_ref):
    c = c_ref[...]
    s = (c * jax.nn.sigmoid(c)).astype(BF16)
    o_ref[...] = _dot(s, w_ref[...].astype(BF16)) + b_ref[...]


def _ada(c, w_ada, b_ada):
    n, d = c.shape
    n_out = w_ada.shape[1]
    bn = 1024
    return pl.pallas_call(
        _ada_kernel,
        grid=(n_out // bn,),
        in_specs=[pl.BlockSpec((n, d), lambda j: (0, 0)),
                  pl.BlockSpec((d, bn), lambda j: (0, j)),
                  pl.BlockSpec((1, bn), lambda j: (0, j))],
        out_specs=pl.BlockSpec((n, bn), lambda j: (0, j)),
        out_shape=jax.ShapeDtypeStruct((n, n_out), F32),
        compiler_params=pltpu.CompilerParams(dimension_semantics=("arbitrary",)),
        name="ada",
    )(c, w_ada, b_ada.reshape(1, n_out))


def _proj_kernel(x_ref, mod_ref, g_ref, w_ref,
                 pr_ref, q_ref, k_ref, v_ref, gt_ref, ptail_ref, ktail_ref, vtail_ref, *, n_tail):
    nb, rows, d = x_ref.shape
    x = x_ref[...]
    mod = mod_ref[...]
    sh = mod[:, :, 0:d]
    sc = mod[:, :, d:2 * d]
    h = (_rms(x) * g_ref[...]) * (1.0 + sc) + sh
    h = h.reshape(nb * rows, d).astype(BF16)

    def col(lo, hi):
        return _dot(h, w_ref[:, lo:hi])

    last = pl.program_id(1) == pl.num_programs(1) - 1
    half = R_PROJ // 2
    for j in range(2):
        p = col(j * half, (j + 1) * half).reshape(nb, rows, half)
        pr_ref[:, :, j * half:(j + 1) * half] = p.astype(BF16)

        @pl.when(last)
        def _():
            ptail_ref[:, :, j * half:(j + 1) * half] = p[:, rows - 8:, :]

    q = col(R_PROJ, R_PROJ + D_A) * ATT_SCALE
    q_ref[...] = q.reshape(nb, rows, D_A).astype(BF16)
    for ref, tail, lo in ((k_ref, ktail_ref, R_PROJ + D_A), (v_ref, vtail_ref, R_PROJ + 2 * D_A)):
        t = col(lo, lo + D_A).reshape(nb, rows, D_A)
        ref[...] = t.astype(BF16)

        @pl.when(last)
        def _():
            tail[...] = t[:, rows - n_tail:, :]

    g0 = R_PROJ + 3 * D_A
    for j in range(2):
        gt = col(g0 + j * d, g0 + (j + 1) * d).reshape(nb, rows, d)
        gt_ref[:, :, j * d:(j + 1) * d] = gt.astype(BF16)


def _proj(x, mod, norm_g, w_in, *, nb, rows, n_tail):
    b, t, d = x.shape
    grid = (b // nb, t // rows)
    tile = lambda n: pl.BlockSpec((nb, rows, n), lambda i, j: (i, j, 0))
    tail = lambda r, n: pl.BlockSpec((nb, r, n), lambda i, j: (i, 0, 0))
    out_shape = (
        jax.ShapeDtypeStruct((b, t, R_PROJ), BF16),
        jax.ShapeDtypeStruct((b, t, D_A), BF16),
        jax.ShapeDtypeStruct((b, t, D_A), BF16),
        jax.ShapeDtypeStruct((b, t, D_A), BF16),
        jax.ShapeDtypeStruct((b, t, 2 * d), BF16),
        jax.ShapeDtypeStruct((b, 8, R_PROJ), F32),
        jax.ShapeDtypeStruct((b, n_tail, D_A), F32),
        jax.ShapeDtypeStruct((b, n_tail, D_A), F32),
    )
    return pl.pallas_call(
        functools.partial(_proj_kernel, n_tail=n_tail),
        grid=grid,
        in_specs=[tile(d),
                  pl.BlockSpec((nb, 1, 6 * d), lambda i, j: (i, 0, 0)),
                  _const_spec((1, 1, d)),
                  _const_spec((d, IN_PROJ))],
        out_specs=(tile(R_PROJ), tile(D_A), tile(D_A), tile(D_A), tile(2 * d),
                   tail(8, R_PROJ), tail(n_tail, D_A), tail(n_tail, D_A)),
        out_shape=out_shape,
        compiler_params=pltpu.CompilerParams(
            dimension_semantics=("arbitrary", "arbitrary"), vmem_limit_bytes=VMEM_LIMIT),
        name="proj",
    )(x, mod, norm_g.reshape(1, 1, d), w_in)


def _pair_blocks(x, m0):
    xb = x.astype(BF16)
    z = jnp.zeros_like(xb)
    return jnp.concatenate([jnp.where(m0, xb, z), jnp.where(m0, z, xb)], axis=0)


def _wkv_pair_chunk(s, r, k, v, kk, b, lw, cum, masks):
    m0, strict, incl, blockdiag = masks
    c = r.shape[0]
    e_cum = jnp.exp(cum)
    r_t = r * e_cum
    kk_t = kk * jnp.exp(cum - lw)
    e_inv = jnp.exp(-cum)
    k_h = k * e_inv
    b_h = b * e_inv
    cum_last = cum[c - 1:c, :]
    dec = jnp.exp(cum_last - cum)
    k_p = k * dec
    b_p = b * dec
    p_last = jnp.exp(cum_last)

    lhs = jnp.concatenate([kk_t, r_t], axis=0).astype(BF16)
    g1 = _dot_nt(lhs, _pair_blocks(k_h, m0))
    g2 = _dot_nt(lhs, _pair_blocks(b_h, m0))
    g3 = _dot_nt(lhs, s.astype(BF16))
    zero = jnp.zeros((c, LANES), F32)
    m_k = jnp.where(strict, g1[:c], zero)
    a_rk = jnp.where(incl, g1[c:], zero)
    n_b = jnp.where(strict, g2[:c], zero)
    a_rb = jnp.where(incl, g2[c:], zero)
    v_blk = _pair_blocks(v, m0)
    rhs = g3[:c] + _dot(m_k.astype(BF16), v_blk)

    y = rhs - _dot(n_b.astype(BF16), _pair_blocks(rhs, m0))
    n_pow = n_b
    for _ in range(int(math.log2(c)) - 1):
        n_pow = _dot(n_pow.astype(BF16), _pair_blocks(n_pow, m0))
        y = y + _dot(n_pow.astype(BF16), _pair_blocks(y, m0))
    u = y

    o = g3[c:] + _dot(jnp.concatenate([a_rk, -a_rb], axis=1).astype(BF16),
                      jnp.concatenate([v_blk, _pair_blocks(u, m0)], axis=0))
    d_s = _dot_tn(jnp.concatenate([v, u], axis=0).astype(BF16),
                  jnp.concatenate([k_p, -b_p], axis=0).astype(BF16))
    s_new = s * p_last + jnp.where(blockdiag, d_s, jnp.zeros_like(d_s))
    return s_new, o


def _wkv_kernel(p_ref, prev0_ref, s0_ref, mu_ref, wwa_ref, w0_ref, a0_ref, gup_ref,
                kk_ref, ka_ref, rk_ref, lng_ref, lnb_ref,
                y_ref, sout_ref,
                s_scr, prev_scr, r_scr, k_scr, v_scr, kks_scr, b_scr, lw_scr, o_scr):
    rows = p_ref.shape[1]
    rows_pad = r_scr.shape[0]
    n_chunks = rows_pad // CHUNK
    t_idx = pl.program_id(1)

    @pl.when(t_idx == 0)
    def _():
        s_scr[...] = s0_ref[0]
        prev_scr[...] = prev0_ref[0]

    p = p_ref[0].astype(F32)
    row = lax.broadcasted_iota(jnp.int32, (rows, 1), 0)
    p_prev = jnp.where(row == 0, prev_scr[0:1, :], pltpu.roll(p, 1, axis=0))
    prev_scr[0:1, :] = p[rows - 1:rows, :]
    xs = p + (p_prev - p) * mu_ref[...]

    r = xs[:, 0:D_R]
    k = xs[:, D_R:2 * D_R]
    v = xs[:, 2 * D_R:3 * D_R]
    x_wa = xs[:, 3 * D_R:3 * D_R + LANES]
    x_g = xs[:, 3 * D_R + LANES:R_PROJ]

    lane = lax.broadcasted_iota(jnp.int32, (rows, LANES), 1)
    wa_in = jnp.where(lane < LORA_W, jnp.tanh(x_wa), x_wa).astype(BF16)
    wa = _dot(wa_in, wwa_ref[...])
    lw = -math.exp(-0.5) * jax.nn.sigmoid(wa[:, 0:D_R] + w0_ref[...])
    a = jax.nn.sigmoid(wa[:, D_R:2 * D_R] + a0_ref[...])
    g = _dot(jax.nn.sigmoid(x_g).astype(BF16), gup_ref[...])

    hr = lax.broadcasted_iota(jnp.int32, (D_R, D_R), 0) // HEAD_DIM
    hc = lax.broadcasted_iota(jnp.int32, (D_R, D_R), 1) // HEAD_DIM
    head_ones = jnp.where(hr == hc, 1.0, 0.0).astype(BF16)

    kk = k * kk_ref[...]
    kk = kk * lax.rsqrt(jnp.maximum(_split_dot(kk * kk, head_ones, 2), 1e-24))
    k2 = k * (1.0 + (a - 1.0) * ka_ref[...])
    b = a * kk

    def put(ref, val):
        ref[0:rows, :] = val
        if rows_pad > rows:
            ref[rows:rows_pad, :] = jnp.zeros((rows_pad - rows, D_R), F32)

    put(r_scr, r)
    put(k_scr, k2)
    put(v_scr, v)
    put(kks_scr, kk)
    put(b_scr, b)
    put(lw_scr, lw)

    ci = lax.broadcasted_iota(jnp.int32, (CHUNK, CHUNK), 0)
    cj = lax.broadcasted_iota(jnp.int32, (CHUNK, CHUNK), 1)
    tri = jnp.where(cj <= ci, 1.0, 0.0).astype(BF16)
    lane_c = lax.broadcasted_iota(jnp.int32, (CHUNK, LANES), 1)
    row_c = lax.broadcasted_iota(jnp.int32, (CHUNK, LANES), 0)
    m0 = lane_c < HEAD_DIM
    jj = jnp.where(m0, lane_c, lane_c - HEAD_DIM)
    br = lax.broadcasted_iota(jnp.int32, (LANES, LANES), 0) < HEAD_DIM
    bc = lax.broadcasted_iota(jnp.int32, (LANES, LANES), 1) < HEAD_DIM
    masks = (m0, jj < row_c, jj <= row_c, br == bc)

    def chunk_body(ci_, carry):
        r0 = pl.multiple_of(ci_ * CHUNK, CHUNK)
        sl = pl.ds(r0, CHUNK)
        lw_c = lw_scr[sl, :]
        cum = _split_dot_lhs(tri, lw_c)
        for pr in range(N_PAIRS):
            ls = slice(pr * LANES, (pr + 1) * LANES)
            s_new, o = _wkv_pair_chunk(
                s_scr[pr], r_scr[sl, ls], k_scr[sl, ls], v_scr[sl, ls], kks_scr[sl, ls],
                b_scr[sl, ls], lw_c[:, ls], cum[:, ls], masks)
            s_scr[pr] = s_new
            o_scr[sl, ls] = o
        return carry

    lax.fori_loop(0, n_chunks, chunk_body, 0)

    o = o_scr[0:rows, :]
    inv_n = 1.0 / HEAD_DIM
    mean = _split_dot(o, head_ones, 2) * inv_n
    cen = o - mean
    var = _split_dot(cen * cen, head_ones, 2) * inv_n
    o_n = cen * lax.rsqrt(var + GN_EPS) * lng_ref[...] + lnb_ref[...]
    bonus = _split_dot(r * k2 * rk_ref[...], head_ones, 2) * v
    y_ref[0] = ((o_n + bonus) * g).astype(BF16)

    @pl.when(t_idx == pl.num_programs(1) - 1)
    def _():
        sout_ref[0] = s_scr[...]


def _split_dot_lhs(tri, x):
    hi = x.astype(BF16)
    r1 = x - hi.astype(F32)
    mid = r1.astype(BF16)
    lo = (r1 - mid.astype(F32)).astype(BF16)
    return _dot(tri, hi) + _dot(tri, mid) + _dot(tri, lo)


def _wkv(p_r, prev0, s0_pairs, wp, *, rows):
    b, t, _ = p_r.shape
    rows_pad = -(-rows // CHUNK) * CHUNK
    grid = (b, t // rows)
    vec = lambda n: _const_spec((1, n))
    scr = lambda: pltpu.VMEM((rows_pad, D_R), F32)
    return pl.pallas_call(
        _wkv_kernel,
        grid=grid,
        in_specs=[pl.BlockSpec((1, rows, R_PROJ), lambda i, j: (i, j, 0)),
                  pl.BlockSpec((1, 8, R_PROJ), lambda i, j: (i, 0, 0)),
                  pl.BlockSpec((1, N_PAIRS, LANES, LANES), lambda i, j: (i, 0, 0, 0)),
                  vec(R_PROJ),
                  _const_spec((LANES, 2 * D_R)),
                  vec(D_R), vec(D_R),
                  _const_spec((LORA_G, D_R)),
                  vec(D_R), vec(D_R), vec(D_R), vec(D_R), vec(D_R)],
        out_specs=(pl.BlockSpec((1, rows, D_R), lambda i, j: (i, j, 0)),
                   pl.BlockSpec((1, N_PAIRS, LANES, LANES), lambda i, j: (i, 0, 0, 0))),
        out_shape=(jax.ShapeDtypeStruct((b, t, D_R), BF16),
                   jax.ShapeDtypeStruct((b, N_PAIRS, LANES, LANES), F32)),
        scratch_shapes=[pltpu.VMEM((N_PAIRS, LANES, LANES), F32),
                        pltpu.VMEM((8, R_PROJ), F32),
                        scr(), scr(), scr(), scr(), scr(), scr(), scr()],
        compiler_params=pltpu.CompilerParams(
            dimension_semantics=("arbitrary", "arbitrary"), vmem_limit_bytes=VMEM_LIMIT),
        name="wkv",
    )(p_r, prev0, s0_pairs, *wp)


def _state_to_pairs(s):
    b = s.shape[0]
    s = s.reshape(b, N_PAIRS, 2, HEAD_DIM, HEAD_DIM)
    z = jnp.zeros_like(s[:, :, 0])
    top = jnp.concatenate([s[:, :, 0], z], axis=-1)
    bot = jnp.concatenate([z, s[:, :, 1]], axis=-1)
    return jnp.concatenate([top, bot], axis=-2)


def _pairs_to_state(sp):
    b = sp.shape[0]
    h0 = sp[:, :, :HEAD_DIM, :HEAD_DIM]
    h1 = sp[:, :, HEAD_DIM:, HEAD_DIM:]
    return jnp.stack([h0, h1], axis=2).reshape(b, N_HEADS, HEAD_DIM, HEAD_DIM)


def _softmax_pv(parts, m0):
    m = None
    for s, _ in parts:
        mi = jnp.max(s, axis=-1, keepdims=True)
        m = mi if m is None else jnp.maximum(m, mi)
    l = None
    pv = None
    for s, vals in parts:
        e = jnp.exp(s - m)
        li = jnp.sum(e, axis=-1, keepdims=True)
        pi = _dot(e.astype(BF16), vals)
        l = li if l is None else l + li
        pv = pi if pv is None else pv + pi
    pv = pv / l
    c = pv.shape[0] // 2
    return jnp.where(m0, pv[:c], pv[c:])


def _pair_queries(q, m0):
    z = jnp.zeros_like(q)
    return jnp.concatenate([jnp.where(m0, q, z), jnp.where(m0, z, q)], axis=0)


def _attn_prompt_kernel(q_ref, kc_ref, kp_ref, vc_ref, vp_ref, bias_ref, y_ref, k_scr, v_scr):
    rows = q_ref.shape[1]
    n_chunks = rows // CHUNK
    t_idx = pl.program_id(1)
    k_scr[0:WINDOW, :] = kp_ref[0]
    k_scr[WINDOW:WINDOW + rows, :] = kc_ref[0]
    v_scr[0:WINDOW, :] = vp_ref[0]
    v_scr[WINDOW:WINDOW + rows, :] = vc_ref[0]
    m0 = lax.broadcasted_iota(jnp.int32, (CHUNK, LANES), 1) < HEAD_DIM
    key_i = lax.broadcasted_iota(jnp.int32, (1, BAND), 1)

    def chunk_body(c, carry):
        r0 = pl.multiple_of(c * CHUNK, CHUNK)
        first_pos = t_idx * rows + r0 - WINDOW
        key_mask = jnp.where(key_i + first_pos >= 0, 0.0, NEG_BIG)
        for pr in range(N_PAIRS):
            ls = slice(pr * LANES, (pr + 1) * LANES)
            lhs = _pair_queries(q_ref[0, pl.ds(r0, CHUNK), ls], m0)
            s = _dot_nt(lhs, k_scr[pl.ds(r0, BAND), ls]) + bias_ref[pr] + key_mask
            y_ref[0, pl.ds(r0, CHUNK), ls] = _softmax_pv(
                [(s, v_scr[pl.ds(r0, BAND), ls])], m0).astype(BF16)
        return carry

    lax.fori_loop(0, n_chunks, chunk_body, 0)


def _attn_prompt(q, k, v, bias_pairs, *, rows):
    b, t, _ = q.shape
    assert rows == WINDOW
    cur = pl.BlockSpec((1, rows, D_A), lambda i, j: (i, j, 0))
    prev = pl.BlockSpec((1, rows, D_A), lambda i, j: (i, jnp.maximum(j - 1, 0), 0))
    return pl.pallas_call(
        _attn_prompt_kernel,
        grid=(b, t // rows),
        in_specs=[cur, cur, prev, cur, prev, _const_spec((N_PAIRS, 2 * CHUNK, BAND))],
        out_specs=cur,
        out_shape=jax.ShapeDtypeStruct((b, t, D_A), BF16),
        scratch_shapes=[pltpu.VMEM((WINDOW + rows, D_A), BF16),
                        pltpu.VMEM((WINDOW + rows, D_A), BF16)],
        compiler_params=pltpu.CompilerParams(
            dimension_semantics=("arbitrary", "arbitrary"), vmem_limit_bytes=VMEM_LIMIT),
        name="attn_prompt",
    )(q, k, k, v, v, bias_pairs)


def _attn_sample_kernel(q_ref, k_ref, v_ref, ck_ref, cv_ref, bias_c_ref, bias_n_ref, y_ref):
    rows = q_ref.shape[1]
    m0 = lax.broadcasted_iota(jnp.int32, (rows, LANES), 1) < HEAD_DIM
    for pr in range(N_PAIRS):
        ls = slice(pr * LANES, (pr + 1) * LANES)
        lhs = _pair_queries(q_ref[0, :, ls], m0)
        s_c = _dot_nt(lhs, ck_ref[0, :, ls].astype(BF16)) + bias_c_ref[pr]
        s_n = _dot_nt(lhs, k_ref[0, :, ls]) + bias_n_ref[pr]
        y_ref[0, :, ls] = _softmax_pv(
            [(s_c, cv_ref[0, :, ls].astype(BF16)), (s_n, v_ref[0, :, ls])], m0).astype(BF16)


def _attn_sample(q, k, v, cache_k, cache_v, bias_c, bias_n):
    b, rows, _ = q.shape
    n_cache = cache_k.shape[1]
    cur = pl.BlockSpec((1, rows, D_A), lambda i: (i, 0, 0))
    cache = pl.BlockSpec((1, n_cache, D_A), lambda i: (i, 0, 0))
    return pl.pallas_call(
        _attn_sample_kernel,
        grid=(b,),
        in_specs=[cur, cur, cur, cache, cache,
                  _const_spec((N_PAIRS, 2 * rows, n_cache)), _const_spec((N_PAIRS, 2 * rows, rows))],
        out_specs=cur,
        out_shape=jax.ShapeDtypeStruct((b, rows, D_A), BF16),
        compiler_params=pltpu.CompilerParams(dimension_semantics=("arbitrary",)),
        name="attn_sample",
    )(q, k, v, cache_k, cache_v, bias_c, bias_n)


def _rel_bias(table, q_pos, k_pos):
    idx = jnp.clip(q_pos[:, None] - k_pos[None, :], -REL_CLIP, REL_CLIP) + REL_CLIP
    bias = table[:, idx].astype(F32)
    h, nq, nk = bias.shape
    return bias.reshape(h // 2, 2 * nq, nk)


def _out_kernel(x_ref, yr_ref, ya_ref, gt_ref, mod_ref, wbr_ref, wba_ref, wo_ref, ng_ref,
                w1_ref, w2_ref, fg_ref, y_ref):
    nb, rows, d = x_ref.shape
    n = nb * rows
    mod = mod_ref[...]
    g1 = mod[:, :, 2 * d:3 * d]
    sh2 = mod[:, :, 3 * d:4 * d]
    sc2 = mod[:, :, 4 * d:5 * d]
    g2 = mod[:, :, 5 * d:6 * d]

    gates = gt_ref[...].reshape(n, 2 * d).astype(F32)
    y_r = yr_ref[...].reshape(n, D_R)
    y_a = ya_ref[...].reshape(n, D_A)
    merged = (jax.nn.sigmoid(gates[:, 0:d]) * _dot(y_r, wbr_ref[...])
              + jax.nn.sigmoid(gates[:, d:2 * d]) * _dot(y_a, wba_ref[...]))
    mix = _dot(merged.astype(BF16), wo_ref[...]).reshape(nb, rows, d)
    x1 = x_ref[...] + g1 * mix

    h2 = ((_rms(x1) * ng_ref[...]) * (1.0 + sc2) + sh2).reshape(n, d).astype(BF16)
    ff_blk = 1024
    acc = None
    for j in range(D_FF // ff_blk):
        mid = _dot(h2, w1_ref[:, j * ff_blk:(j + 1) * ff_blk])
        act = jnp.square(jnp.maximum(mid, 0.0)).astype(BF16)
        part = _dot(act, w2_ref[j * ff_blk:(j + 1) * ff_blk, :])
        acc = part if acc is None else acc + part
    x2 = x1 + g2 * acc.reshape(nb, rows, d)
    y_ref[...] = _rms(x2) * fg_ref[...]


def _out(x, y_r, y_a, gates, mod, w_br_r, w_br_a, w_out, norm_g, w_ff1, w_ff2, final_g, *, nb, rows):
    b, t, d = x.shape
    tile = lambda n: pl.BlockSpec((nb, rows, n), lambda i, j: (i, j, 0))
    return pl.pallas_call(
        _out_kernel,
        grid=(b // nb, t // rows),
        in_specs=[tile(d), tile(D_R), tile(D_A), tile(2 * d),
                  pl.BlockSpec((nb, 1, 6 * d), lambda i, j: (i, 0, 0)),
                  _const_spec((D_R, d)), _const_spec((D_A, d)), _const_spec((d, d)),
                  _const_spec((1, 1, d)),
                  _const_spec((d, D_FF)), _const_spec((D_FF, d)),
                  _const_spec((1, 1, d))],
        out_specs=tile(d),
        out_shape=jax.ShapeDtypeStruct((b, t, d), F32),
        compiler_params=pltpu.CompilerParams(
            dimension_semantics=("arbitrary", "arbitrary"), vmem_limit_bytes=VMEM_LIMIT),
        name="out",
    )(x, y_r, y_a, gates, mod, w_br_r, w_br_a, w_out, norm_g.reshape(1, 1, d),
      w_ff1, w_ff2, final_g.reshape(1, 1, d))


def _group(x, mod, prev0, s0, attend, W, *, nb, rows, n_tail):
    b, t, d = x.shape
    p_r, q, k, v, gates, p_tail, k_tail, v_tail = _proj(
        x, mod, W["norm_mix_g"], W["w_in"], nb=nb, rows=rows, n_tail=n_tail)
    y_r, s_pairs = _wkv(p_r, prev0, _state_to_pairs(s0), W["wkv"], rows=rows)
    y_a = attend(q, k, v)
    y = _out(x, y_r, y_a, gates, mod, W["w_br_r"], W["w_br_a"], W["w_out"], W["norm_mlp_g"],
             W["w_ff1"], W["w_ff2"], W["final_norm_g"], nb=nb, rows=rows)
    return (y, _pairs_to_state(s_pairs)[None], p_tail[None, :, 7, :],
            k_tail.reshape(1, b, n_tail, N_HEADS, HEAD_DIM),
            v_tail.reshape(1, b, n_tail, N_HEADS, HEAD_DIM))


def kernel(x_prompt, x_sample, c_prompt, c_sample, state_rwkv_wkv, state_rwkv_shift, cache_att_k, cache_att_v, w_ada, b_ada, norm_mix_g, w_in, mu_shift, w_lora_up, w0, a_lora_up, a0, g_lora_up, k_k, k_a, r_k, lnx_g, lnx_b, rel_table, w_br_r, w_br_a, w_out, norm_mlp_g, w_ff1, w_ff2, final_norm_g):
    bp, tp, d = x_prompt.shape
    bs, ts, _ = x_sample.shape
    l = 0
    row = lambda a: a.reshape(1, -1)
    zl = jnp.zeros((LORA_W, D_R), F32)
    w_wa = jnp.concatenate([jnp.concatenate([w_lora_up[l], zl], axis=1),
                            jnp.concatenate([zl, a_lora_up[l]], axis=1)], axis=0).astype(BF16)
    W = dict(
        norm_mix_g=norm_mix_g[l], w_in=w_in[l].astype(BF16),
        wkv=(row(mu_shift[l]), w_wa, row(w0[l]), row(a0[l]), g_lora_up[l].astype(BF16),
             row(k_k[l]), row(k_a[l]), row(r_k[l]), row(lnx_g[l]), row(lnx_b[l])),
        w_br_r=w_br_r[l].astype(BF16), w_br_a=w_br_a[l].astype(BF16), w_out=w_out[l].astype(BF16),
        norm_mlp_g=norm_mlp_g[l], w_ff1=w_ff1[l].astype(BF16), w_ff2=w_ff2[l].astype(BF16),
        final_norm_g=final_norm_g)

    mod = _ada(jnp.concatenate([c_prompt, c_sample], axis=0), w_ada[l], b_ada[l])
    mod_p = mod[:bp, None, :]
    mod_s = mod[bp:, None, :]

    bias_p = _rel_bias(rel_table[l], jnp.arange(CHUNK), jnp.arange(BAND) - WINDOW)
    attend_p = lambda q, k, v: _attn_prompt(q, k, v, bias_p, rows=WINDOW)
    n_keep = min(WINDOW, tp)
    y_p, p_wkv, p_shift, p_k, p_v = _group(
        x_prompt, mod_p, jnp.zeros((bp, 8, R_PROJ), F32),
        jnp.zeros((bp, N_HEADS, HEAD_DIM, HEAD_DIM), F32), attend_p, W,
        nb=1, rows=WINDOW, n_tail=n_keep)

    n_cache = cache_att_k.shape[2]
    q_pos = PAST_LEN + jnp.arange(ts)
    k_pos = jnp.concatenate([PAST_LEN - n_cache + jnp.arange(n_cache), q_pos])
    bias_s = _rel_bias(rel_table[l], q_pos, k_pos)
    ck = cache_att_k[l].reshape(bs, n_cache, D_A)
    cv = cache_att_v[l].reshape(bs, n_cache, D_A)
    attend_s = lambda q, k, v: _attn_sample(q, k, v, ck, cv, bias_s[:, :, :n_cache], bias_s[:, :, n_cache:])
    prev_s = jnp.broadcast_to(state_rwkv_shift[l][:, None, :], (bs, 8, R_PROJ))
    y_s, s_wkv, s_shift, s_k, s_v = _group(
        x_sample, mod_s, prev_s, state_rwkv_wkv[l], attend_s, W, nb=bs, rows=ts, n_tail=ts)

    return (y_p, y_s, p_wkv, p_shift, p_k, p_v, s_wkv, s_shift, s_k, s_v)
```

```python
import functools
import math

import jax
import jax.numpy as jnp
from jax import lax
from jax.experimental import pallas as pl
from jax.experimental.pallas import tpu as pltpu

D_MODEL = 1024
HEAD_DIM = 64
N_HEADS = 8
D_R = N_HEADS * HEAD_DIM
D_A = N_HEADS * HEAD_DIM
LORA_W = 64
LORA_A = 64
LORA_G = 128
R_PROJ = 3 * D_R + LORA_W + LORA_A + LORA_G
CHUNK = 64
BAND_CHUNKS = 8
WINDOW = BAND_CHUNKS * CHUNK
BAND = WINDOW + CHUNK
REL_CLIP = 128
D_FF = 4 * D_MODEL
IN_PROJ = R_PROJ + 3 * D_A + 2 * D_MODEL
NORM_EPS = 1e-6
GN_EPS = 64e-5
ATT_SCALE = HEAD_DIM ** -0.5
PAST_LEN = 1024

LANES = 128
N_PAIRS = N_HEADS // 2
NEG_BIG = -1e30
HEAD_SUM_TERMS = 1
WKV_GROUP_CHUNKS = 2
VMEM_LIMIT = 56 * 1024 * 1024

F32 = jnp.float32
BF16 = jnp.bfloat16


def _dot(a, b):
    return lax.dot_general(a, b, (((1,), (0,)), ((), ())), preferred_element_type=F32)


def _dot_nt(a, b):
    return lax.dot_general(a, b, (((1,), (1,)), ((), ())), preferred_element_type=F32)


def _dot_tn(a, b):
    return lax.dot_general(a, b, (((0,), (0,)), ((), ())), preferred_element_type=F32)


def _split_dot(x, w, terms):
    acc = None
    rem = x
    for _ in range(terms):
        piece = rem.astype(BF16)
        part = _dot(piece, w)
        acc = part if acc is None else acc + part
        rem = rem - piece.astype(F32)
    return acc


def _rms(x):
    return x * lax.rsqrt(jnp.mean(x * x, axis=-1, keepdims=True) + NORM_EPS)


def _const_spec(shape):
    nd = len(shape)
    return pl.BlockSpec(shape, lambda *_: (0,) * nd, pipeline_mode=pl.Buffered(1))


def _ada_kernel(c_ref, w_ref, b_ref, o_ref):
    c = c_ref[...]
    s = (c * jax.nn.sigmoid(c)).astype(BF16)
    o_ref[...] = _dot(s, w_ref[...].astype(BF16)) + b_ref[...]


def _ada(c, w_ada, b_ada):
    n, d = c.shape
    n_out = w_ada.shape[1]
    bn = 1024
    return pl.pallas_call(
        _ada_kernel,
        grid=(n_out // bn,),
        in_specs=[pl.BlockSpec((n, d), lambda j: (0, 0)),
                  pl.BlockSpec((d, bn), lambda j: (0, j)),
                  pl.BlockSpec((1, bn), lambda j: (0, j))],
        out_specs=pl.BlockSpec((n, bn), lambda j: (0, j)),
        out_shape=jax.ShapeDtypeStruct((n, n_out), F32),
        compiler_params=pltpu.CompilerParams(dimension_semantics=("arbitrary",)),
        name="ada",
    )(c, w_ada, b_ada.reshape(1, n_out))


def _proj_kernel(x_ref, mod_ref, g_ref, w_ref,
                 pr_ref, q_ref, k_ref, v_ref, gt_ref, ptail_ref, ktail_ref, vtail_ref, *, n_tail):
    nb, rows, d = x_ref.shape
    x = x_ref[...]
    mod = mod_ref[...]
    sh = mod[:, :, 0:d]
    sc = mod[:, :, d:2 * d]
    h = (_rms(x) * g_ref[...]) * (1.0 + sc) + sh
    h = h.reshape(nb * rows, d).astype(BF16)

    def col(lo, hi):
        return _dot(h, w_ref[:, lo:hi])

    last = pl.program_id(1) == pl.num_programs(1) - 1
    half = R_PROJ // 2
    for j in range(2):
        p = col(j * half, (j + 1) * half).reshape(nb, rows, half)
        pr_ref[:, :, j * half:(j + 1) * half] = p.astype(BF16)

        @pl.when(last)
        def _():
            ptail_ref[:, :, j * half:(j + 1) * half] = p[:, rows - 8:, :]

    q = col(R_PROJ, R_PROJ + D_A) * ATT_SCALE
    q_ref[...] = q.reshape(nb, rows, D_A).astype(BF16)
    for ref, tail, lo in ((k_ref, ktail_ref, R_PROJ + D_A), (v_ref, vtail_ref, R_PROJ + 2 * D_A)):
        t = col(lo, lo + D_A).reshape(nb, rows, D_A)
        ref[...] = t.astype(BF16)

        @pl.when(last)
        def _():
            tail[...] = t[:, rows - n_tail:, :]

    g0 = R_PROJ + 3 * D_A
    for j in range(2):
        gt = col(g0 + j * d, g0 + (j + 1) * d).reshape(nb, rows, d)
        gt_ref[:, :, j * d:(j + 1) * d] = gt.astype(BF16)


def _proj(x, mod, norm_g, w_in, *, nb, rows, n_tail):
    b, t, d = x.shape
    grid = (b // nb, t // rows)
    tile = lambda n: pl.BlockSpec((nb, rows, n), lambda i, j: (i, j, 0))
    tail = lambda r, n: pl.BlockSpec((nb, r, n), lambda i, j: (i, 0, 0))
    out_shape = (
        jax.ShapeDtypeStruct((b, t, R_PROJ), BF16),
        jax.ShapeDtypeStruct((b, t, D_A), BF16),
        jax.ShapeDtypeStruct((b, t, D_A), BF16),
        jax.ShapeDtypeStruct((b, t, D_A), BF16),
        jax.ShapeDtypeStruct((b, t, 2 * d), BF16),
        jax.ShapeDtypeStruct((b, 8, R_PROJ), F32),
        jax.ShapeDtypeStruct((b, n_tail, D_A), F32),
        jax.ShapeDtypeStruct((b, n_tail, D_A), F32),
    )
    return pl.pallas_call(
        functools.partial(_proj_kernel, n_tail=n_tail),
        grid=grid,
        in_specs=[tile(d),
                  pl.BlockSpec((nb, 1, 6 * d), lambda i, j: (i, 0, 0)),
                  _const_spec((1, 1, d)),
                  _const_spec((d, IN_PROJ))],
        out_specs=(tile(R_PROJ), tile(D_A), tile(D_A), tile(D_A), tile(2 * d),
                   tail(8, R_PROJ), tail(n_tail, D_A), tail(n_tail, D_A)),
        out_shape=out_shape,
        compiler_params=pltpu.CompilerParams(
            dimension_semantics=("arbitrary", "arbitrary"), vmem_limit_bytes=VMEM_LIMIT),
        name="proj",
    )(x, mod, norm_g.reshape(1, 1, d), w_in)


def _pair_blocks(x, m0):
    xb = x.astype(BF16)
    z = jnp.zeros_like(xb)
    return jnp.concatenate([jnp.where(m0, xb, z), jnp.where(m0, z, xb)], axis=0)


def _wkv_chunk_local(units, masks):
    m0, strict, incl, blockdiag = masks
    c = CHUNK
    zero = jnp.zeros((c, LANES), F32)
    zero2 = jnp.zeros((LANES, LANES), F32)
    pre = []
    for r, k, v, kk, b, lw, cum in units:
        r_t = r * jnp.exp(cum)
        kk_t = kk * jnp.exp(cum - lw)
        e_inv = jnp.exp(-cum)
        cum_last = cum[c - 1:c, :]
        dec = jnp.exp(cum_last - cum)
        pre.append(dict(
            r_t=r_t, kk_t=kk_t, v=v, lhs=jnp.concatenate([kk_t, r_t], axis=0).astype(BF16),
            k_h=_pair_blocks(k * e_inv, m0), b_h=_pair_blocks(b * e_inv, m0),
            v_blk=_pair_blocks(v, m0), k_p=k * dec, b_p=b * dec, p_last=jnp.exp(cum_last)))
    g1 = [_dot_nt(p["lhs"], p["k_h"]) for p in pre]
    g2 = [_dot_nt(p["lhs"], p["b_h"]) for p in pre]
    m_k = [jnp.where(strict, g[:c], zero).astype(BF16) for g in g1]
    a_rk = [jnp.where(incl, g[c:], zero) for g in g1]
    n_pow = [jnp.where(strict, g[:c], zero) for g in g2]
    a_rb = [jnp.where(incl, g[c:], zero) for g in g2]
    m_kv = [_dot(m, p["v_blk"]) for m, p in zip(m_k, pre)]

    def blocks2(y):
        return jnp.concatenate([_pair_blocks(y[:, :LANES], m0), _pair_blocks(y[:, LANES:], m0)], axis=1)

    y = [jnp.concatenate([mv, p["kk_t"]], axis=1) for mv, p in zip(m_kv, pre)]
    y = [yi - _dot(n.astype(BF16), blocks2(yi)) for yi, n in zip(y, n_pow)]
    for _ in range(int(math.log2(c)) - 1):
        n_pow = [_dot(n.astype(BF16), _pair_blocks(n, m0)) for n in n_pow]
        y = [yi + _dot(n.astype(BF16), blocks2(yi)) for yi, n in zip(y, n_pow)]
    u_loc = [yi[:, :LANES] for yi in y]
    q = [yi[:, LANES:] for yi in y]

    o_loc = [_dot(jnp.concatenate([ak, ab], axis=1).astype(BF16),
                  jnp.concatenate([p["v_blk"], _pair_blocks(-u, m0)], axis=0))
             for ak, ab, u, p in zip(a_rk, a_rb, u_loc, pre)]
    r_eff = [p["r_t"] - _dot(ab.astype(BF16), _pair_blocks(qi, m0)) for p, ab, qi in zip(pre, a_rb, q)]
    d_s = [jnp.where(blockdiag,
                     _dot_tn(jnp.concatenate([p["v"], u], axis=0).astype(BF16),
                             jnp.concatenate([p["k_p"], -p["b_p"]], axis=0).astype(BF16)), zero2)
           for p, u in zip(pre, u_loc)]
    qtb = [jnp.where(blockdiag, _dot_tn(qi.astype(BF16), p["b_p"].astype(BF16)), zero2)
           for p, qi in zip(pre, q)]
    return [(re, ol, qb, ds, p["p_last"]) for re, ol, qb, ds, p in zip(r_eff, o_loc, qtb, d_s, pre)]


def _wkv_kernel(p_ref, prev0_ref, s0_ref, mu_ref, wwa_ref, w0_ref, a0_ref, gup_ref,
                kk_ref, ka_ref, rk_ref, lng_ref, lnb_ref,
                y_ref, sout_ref,
                s_scr, prev_scr, r_scr, k_scr, v_scr, kks_scr, b_scr, lw_scr, o_scr,
                reff_scr, qtb_scr, ds_scr, pc_scr):
    rows = p_ref.shape[1]
    rows_pad = r_scr.shape[0]
    n_chunks = rows_pad // CHUNK
    t_idx = pl.program_id(1)

    @pl.when(t_idx == 0)
    def _():
        s_scr[...] = s0_ref[0]
        prev_scr[...] = prev0_ref[0]

    p = p_ref[0].astype(F32)
    row = lax.broadcasted_iota(jnp.int32, (rows, 1), 0)
    p_prev = jnp.where(row == 0, prev_scr[0:1, :], pltpu.roll(p, 1, axis=0))
    prev_scr[0:1, :] = p[rows - 1:rows, :]
    xs = p + (p_prev - p) * mu_ref[...]

    r = xs[:, 0:D_R]
    k = xs[:, D_R:2 * D_R]
    v = xs[:, 2 * D_R:3 * D_R]
    x_wa = xs[:, 3 * D_R:3 * D_R + LANES]
    x_g = xs[:, 3 * D_R + LANES:R_PROJ]

    lane = lax.broadcasted_iota(jnp.int32, (rows, LANES), 1)
    wa_in = jnp.where(lane < LORA_W, jnp.tanh(x_wa), x_wa).astype(BF16)
    wa = _dot(wa_in, wwa_ref[...])
    lw = -math.exp(-0.5) * jax.nn.sigmoid(wa[:, 0:D_R] + w0_ref[...])
    a = jax.nn.sigmoid(wa[:, D_R:2 * D_R] + a0_ref[...])
    g = _dot(jax.nn.sigmoid(x_g).astype(BF16), gup_ref[...])

    hr = lax.broadcasted_iota(jnp.int32, (D_R, D_R), 0) // HEAD_DIM
    hc = lax.broadcasted_iota(jnp.int32, (D_R, D_R), 1) // HEAD_DIM
    head_ones = jnp.where(hr == hc, 1.0, 0.0).astype(BF16)

    kk = k * kk_ref[...]
    kk = kk * lax.rsqrt(jnp.maximum(_split_dot(kk * kk, head_ones, HEAD_SUM_TERMS), 1e-24))
    k2 = k * (1.0 + (a - 1.0) * ka_ref[...])
    b = a * kk

    def put(ref, val):
        ref[0:rows, :] = val
        if rows_pad > rows:
            ref[rows:rows_pad, :] = jnp.zeros((rows_pad - rows, D_R), F32)

    put(r_scr, r)
    put(k_scr, k2)
    put(v_scr, v)
    put(kks_scr, kk)
    put(b_scr, b)
    put(lw_scr, lw)

    ci = lax.broadcasted_iota(jnp.int32, (CHUNK, CHUNK), 0)
    cj = lax.broadcasted_iota(jnp.int32, (CHUNK, CHUNK), 1)
    tri = jnp.where(cj <= ci, 1.0, 0.0).astype(BF16)
    lane_c = lax.broadcasted_iota(jnp.int32, (CHUNK, LANES), 1)
    row_c = lax.broadcasted_iota(jnp.int32, (CHUNK, LANES), 0)
    m0 = lane_c < HEAD_DIM
    jj = jnp.where(m0, lane_c, lane_c - HEAD_DIM)
    br = lax.broadcasted_iota(jnp.int32, (LANES, LANES), 0) < HEAD_DIM
    bc = lax.broadcasted_iota(jnp.int32, (LANES, LANES), 1) < HEAD_DIM
    masks = (m0, jj < row_c, jj <= row_c, br == bc)

    group = math.gcd(n_chunks, WKV_GROUP_CHUNKS)

    def local_body(gi, carry):
        units = []
        for cc in range(group):
            r0 = pl.multiple_of((gi * group + cc) * CHUNK, CHUNK)
            sl = pl.ds(r0, CHUNK)
            lw_c = lw_scr[sl, :]
            cum = _split_dot_lhs(tri, lw_c)
            for pr in range(N_PAIRS):
                ls = slice(pr * LANES, (pr + 1) * LANES)
                units.append((r_scr[sl, ls], k_scr[sl, ls], v_scr[sl, ls], kks_scr[sl, ls],
                              b_scr[sl, ls], lw_c[:, ls], cum[:, ls]))
        res = _wkv_chunk_local(units, masks)
        for cc in range(group):
            ci_ = gi * group + cc
            sl = pl.ds(pl.multiple_of(ci_ * CHUNK, CHUNK), CHUNK)
            for pr in range(N_PAIRS):
                r_eff, o_loc, qtb, d_s, p_last = res[cc * N_PAIRS + pr]
                u = ci_ * N_PAIRS + pr
                reff_scr[u] = r_eff.astype(BF16)
                o_scr[sl, pr * LANES:(pr + 1) * LANES] = o_loc
                qtb_scr[u] = qtb.astype(BF16)
                ds_scr[u] = d_s
                pc_scr[u] = jnp.broadcast_to(p_last, (8, LANES))
        return carry

    lax.fori_loop(0, n_chunks // group, local_body, 0)

    def state_body(ci_, carry):
        sl = pl.ds(pl.multiple_of(ci_ * CHUNK, CHUNK), CHUNK)
        s_old = [s_scr[pr] for pr in range(N_PAIRS)]
        s_bf = [s.astype(BF16) for s in s_old]
        us = [ci_ * N_PAIRS + pr for pr in range(N_PAIRS)]
        o_add = [_dot_nt(reff_scr[u], sb) for u, sb in zip(us, s_bf)]
        s_mix = [_dot(sb, qtb_scr[u]) for u, sb in zip(us, s_bf)]
        for pr in range(N_PAIRS):
            ls = slice(pr * LANES, (pr + 1) * LANES)
            o_scr[sl, ls] = o_scr[sl, ls] + o_add[pr]
            s_scr[pr] = s_old[pr] * pc_scr[us[pr]][0:1, :] + ds_scr[us[pr]] - s_mix[pr]
        return carry

    lax.fori_loop(0, n_chunks, state_body, 0)

    o = o_scr[0:rows, :]
    inv_n = 1.0 / HEAD_DIM
    mean = _split_dot(o, head_ones, HEAD_SUM_TERMS) * inv_n
    cen = o - mean
    var = _split_dot(cen * cen, head_ones, HEAD_SUM_TERMS) * inv_n
    o_n = cen * lax.rsqrt(var + GN_EPS) * lng_ref[...] + lnb_ref[...]
    bonus = _split_dot(r * k2 * rk_ref[...], head_ones, HEAD_SUM_TERMS) * v
    y_ref[0] = ((o_n + bonus) * g).astype(BF16)

    @pl.when(t_idx == pl.num_programs(1) - 1)
    def _():
        sout_ref[0] = s_scr[...]


def _split_dot_lhs(tri, x):
    hi = x.astype(BF16)
    r1 = x - hi.astype(F32)
    mid = r1.astype(BF16)
    lo = (r1 - mid.astype(F32)).astype(BF16)
    return _dot(tri, hi) + _dot(tri, mid) + _dot(tri, lo)


def _wkv(p_r, prev0, s0_pairs, wp, *, rows):
    b, t, _ = p_r.shape
    rows_pad = -(-rows // CHUNK) * CHUNK
    n_units = rows_pad // CHUNK * N_PAIRS
    grid = (b, t // rows)
    vec = lambda n: _const_spec((1, n))
    scr = lambda: pltpu.VMEM((rows_pad, D_R), F32)
    return pl.pallas_call(
        _wkv_kernel,
        grid=grid,
        in_specs=[pl.BlockSpec((1, rows, R_PROJ), lambda i, j: (i, j, 0)),
                  pl.BlockSpec((1, 8, R_PROJ), lambda i, j: (i, 0, 0)),
                  pl.BlockSpec((1, N_PAIRS, LANES, LANES), lambda i, j: (i, 0, 0, 0)),
                  vec(R_PROJ),
                  _const_spec((LANES, 2 * D_R)),
                  vec(D_R), vec(D_R),
                  _const_spec((LORA_G, D_R)),
                  vec(D_R), vec(D_R), vec(D_R), vec(D_R), vec(D_R)],
        out_specs=(pl.BlockSpec((1, rows, D_R), lambda i, j: (i, j, 0)),
                   pl.BlockSpec((1, N_PAIRS, LANES, LANES), lambda i, j: (i, 0, 0, 0))),
        out_shape=(jax.ShapeDtypeStruct((b, t, D_R), BF16),
                   jax.ShapeDtypeStruct((b, N_PAIRS, LANES, LANES), F32)),
        scratch_shapes=[pltpu.VMEM((N_PAIRS, LANES, LANES), F32),
                        pltpu.VMEM((8, R_PROJ), F32),
                        scr(), scr(), scr(), scr(), scr(), scr(), scr(),
                        pltpu.VMEM((n_units, CHUNK, LANES), BF16),
                        pltpu.VMEM((n_units, LANES, LANES), BF16),
                        pltpu.VMEM((n_units, LANES, LANES), F32),
                        pltpu.VMEM((n_units, 8, LANES), F32)],
        compiler_params=pltpu.CompilerParams(
            dimension_semantics=("arbitrary", "arbitrary"), vmem_limit_bytes=VMEM_LIMIT),
        name="wkv",
    )(p_r, prev0, s0_pairs, *wp)


def _state_to_pairs(s):
    b = s.shape[0]
    s = s.reshape(b, N_PAIRS, 2, HEAD_DIM, HEAD_DIM)
    z = jnp.zeros_like(s[:, :, 0])
    top = jnp.concatenate([s[:, :, 0], z], axis=-1)
    bot = jnp.concatenate([z, s[:, :, 1]], axis=-1)
    return jnp.concatenate([top, bot], axis=-2)


def _pairs_to_state(sp):
    b = sp.shape[0]
    h0 = sp[:, :, :HEAD_DIM, :HEAD_DIM]
    h1 = sp[:, :, HEAD_DIM:, HEAD_DIM:]
    return jnp.stack([h0, h1], axis=2).reshape(b, N_HEADS, HEAD_DIM, HEAD_DIM)


def _softmax_pv(parts, m0):
    m = None
    for s, _ in parts:
        mi = jnp.max(s, axis=-1, keepdims=True)
        m = mi if m is None else jnp.maximum(m, mi)
    l = None
    pv = None
    for s, vals in parts:
        e = jnp.exp(s - m)
        li = jnp.sum(e, axis=-1, keepdims=True)
        pi = _dot(e.astype(BF16), vals)
        l = li if l is None else l + li
        pv = pi if pv is None else pv + pi
    pv = pv / l
    c = pv.shape[0] // 2
    return jnp.where(m0, pv[:c], pv[c:])


def _pair_queries(q, m0):
    z = jnp.zeros_like(q)
    return jnp.concatenate([jnp.where(m0, q, z), jnp.where(m0, z, q)], axis=0)


def _attn_prompt_kernel(q_ref, kc_ref, kp_ref, vc_ref, vp_ref, bias_ref, y_ref, k_scr, v_scr):
    rows = q_ref.shape[1]
    n_chunks = rows // CHUNK
    t_idx = pl.program_id(1)
    k_scr[0:WINDOW, :] = kp_ref[0]
    k_scr[WINDOW:WINDOW + rows, :] = kc_ref[0]
    v_scr[0:WINDOW, :] = vp_ref[0]
    v_scr[WINDOW:WINDOW + rows, :] = vc_ref[0]
    m0 = lax.broadcasted_iota(jnp.int32, (CHUNK, LANES), 1) < HEAD_DIM
    key_i = lax.broadcasted_iota(jnp.int32, (1, BAND), 1)

    def chunk_body(c, carry):
        r0 = pl.multiple_of(c * CHUNK, CHUNK)
        first_pos = t_idx * rows + r0 - WINDOW
        key_mask = jnp.where(key_i + first_pos >= 0, 0.0, NEG_BIG)
        for pr in range(N_PAIRS):
            ls = slice(pr * LANES, (pr + 1) * LANES)
            lhs = _pair_queries(q_ref[0, pl.ds(r0, CHUNK), ls], m0)
            s = _dot_nt(lhs, k_scr[pl.ds(r0, BAND), ls]) + bias_ref[pr] + key_mask
            y_ref[0, pl.ds(r0, CHUNK), ls] = _softmax_pv(
                [(s, v_scr[pl.ds(r0, BAND), ls])], m0).astype(BF16)
        return carry

    lax.fori_loop(0, n_chunks, chunk_body, 0)


def _attn_prompt(q, k, v, bias_pairs, *, rows):
    b, t, _ = q.shape
    assert rows == WINDOW
    cur = pl.BlockSpec((1, rows, D_A), lambda i, j: (i, j, 0))
    prev = pl.BlockSpec((1, rows, D_A), lambda i, j: (i, jnp.maximum(j - 1, 0), 0))
    return pl.pallas_call(
        _attn_prompt_kernel,
        grid=(b, t // rows),
        in_specs=[cur, cur, prev, cur, prev, _const_spec((N_PAIRS, 2 * CHUNK, BAND))],
        out_specs=cur,
        out_shape=jax.ShapeDtypeStruct((b, t, D_A), BF16),
        scratch_shapes=[pltpu.VMEM((WINDOW + rows, D_A), BF16),
                        pltpu.VMEM((WINDOW + rows, D_A), BF16)],
        compiler_params=pltpu.CompilerParams(
            dimension_semantics=("arbitrary", "arbitrary"), vmem_limit_bytes=VMEM_LIMIT),
        name="attn_prompt",
    )(q, k, k, v, v, bias_pairs)


def _attn_sample_kernel(q_ref, k_ref, v_ref, ck_ref, cv_ref, bias_c_ref, bias_n_ref, y_ref):
    rows = q_ref.shape[1]
    m0 = lax.broadcasted_iota(jnp.int32, (rows, LANES), 1) < HEAD_DIM
    for pr in range(N_PAIRS):
        ls = slice(pr * LANES, (pr + 1) * LANES)
        lhs = _pair_queries(q_ref[0, :, ls], m0)
        s_c = _dot_nt(lhs, ck_ref[0, :, ls].astype(BF16)) + bias_c_ref[pr]
        s_n = _dot_nt(lhs, k_ref[0, :, ls]) + bias_n_ref[pr]
        y_ref[0, :, ls] = _softmax_pv(
            [(s_c, cv_ref[0, :, ls].astype(BF16)), (s_n, v_ref[0, :, ls])], m0).astype(BF16)


def _attn_sample(q, k, v, cache_k, cache_v, bias_c, bias_n):
    b, rows, _ = q.shape
    n_cache = cache_k.shape[1]
    cur = pl.BlockSpec((1, rows, D_A), lambda i: (i, 0, 0))
    cache = pl.BlockSpec((1, n_cache, D_A), lambda i: (i, 0, 0))
    return pl.pallas_call(
        _attn_sample_kernel,
        grid=(b,),
        in_specs=[cur, cur, cur, cache, cache,
                  _const_spec((N_PAIRS, 2 * rows, n_cache)), _const_spec((N_PAIRS, 2 * rows, rows))],
        out_specs=cur,
        out_shape=jax.ShapeDtypeStruct((b, rows, D_A), BF16),
        compiler_params=pltpu.CompilerParams(dimension_semantics=("arbitrary",)),
        name="attn_sample",
    )(q, k, v, cache_k, cache_v, bias_c, bias_n)


def _rel_bias(table, q_pos, k_pos):
    idx = jnp.clip(q_pos[:, None] - k_pos[None, :], -REL_CLIP, REL_CLIP) + REL_CLIP
    bias = table[:, idx].astype(F32)
    h, nq, nk = bias.shape
    return bias.reshape(h // 2, 2 * nq, nk)


def _out_kernel(x_ref, yr_ref, ya_ref, gt_ref, mod_ref, wbr_ref, wba_ref, wo_ref, ng_ref,
                w1_ref, w2_ref, fg_ref, y_ref):
    nb, rows, d = x_ref.shape
    n = nb * rows
    mod = mod_ref[...]
    g1 = mod[:, :, 2 * d:3 * d]
    sh2 = mod[:, :, 3 * d:4 * d]
    sc2 = mod[:, :, 4 * d:5 * d]
    g2 = mod[:, :, 5 * d:6 * d]

    gates = gt_ref[...].reshape(n, 2 * d).astype(F32)
    y_r = yr_ref[...].reshape(n, D_R)
    y_a = ya_ref[...].reshape(n, D_A)
    merged = (jax.nn.sigmoid(gates[:, 0:d]) * _dot(y_r, wbr_ref[...])
              + jax.nn.sigmoid(gates[:, d:2 * d]) * _dot(y_a, wba_ref[...]))
    mix = _dot(merged.astype(BF16), wo_ref[...]).reshape(nb, rows, d)
    x1 = x_ref[...] + g1 * mix

    h2 = ((_rms(x1) * ng_ref[...]) * (1.0 + sc2) + sh2).reshape(n, d).astype(BF16)
    ff_blk = 1024
    acc = None
    for j in range(D_FF // ff_blk):
        mid = _dot(h2, w1_ref[:, j * ff_blk:(j + 1) * ff_blk])
        act = jnp.square(jnp.maximum(mid, 0.0)).astype(BF16)
        part = _dot(act, w2_ref[j * ff_blk:(j + 1) * ff_blk, :])
        acc = part if acc is None else acc + part
    x2 = x1 + g2 * acc.reshape(nb, rows, d)
    y_ref[...] = _rms(x2) * fg_ref[...]


def _out(x, y_r, y_a, gates, mod, w_br_r, w_br_a, w_out, norm_g, w_ff1, w_ff2, final_g, *, nb, rows):
    b, t, d = x.shape
    tile = lambda n: pl.BlockSpec((nb, rows, n), lambda i, j: (i, j, 0))
    return pl.pallas_call(
        _out_kernel,
        grid=(b // nb, t // rows),
        in_specs=[tile(d), tile(D_R), tile(D_A), tile(2 * d),
                  pl.BlockSpec((nb, 1, 6 * d), lambda i, j: (i, 0, 0)),
                  _const_spec((D_R, d)), _const_spec((D_A, d)), _const_spec((d, d)),
                  _const_spec((1, 1, d)),
                  _const_spec((d, D_FF)), _const_spec((D_FF, d)),
                  _const_spec((1, 1, d))],
        out_specs=tile(d),
        out_shape=jax.ShapeDtypeStruct((b, t, d), F32),
        compiler_params=pltpu.CompilerParams(
            dimension_semantics=("arbitrary", "arbitrary"), vmem_limit_bytes=VMEM_LIMIT),
        name="out",
    )(x, y_r, y_a, gates, mod, w_br_r, w_br_a, w_out, norm_g.reshape(1, 1, d),
      w_ff1, w_ff2, final_g.reshape(1, 1, d))


def _group(x, mod, prev0, s0, attend, W, *, nb, rows, n_tail):
    b, t, d = x.shape
    p_r, q, k, v, gates, p_tail, k_tail, v_tail = _proj(
        x, mod, W["norm_mix_g"], W["w_in"], nb=nb, rows=rows, n_tail=n_tail)
    y_r, s_pairs = _wkv(p_r, prev0, _state_to_pairs(s0), W["wkv"], rows=rows)
    y_a = attend(q, k, v)
    y = _out(x, y_r, y_a, gates, mod, W["w_br_r"], W["w_br_a"], W["w_out"], W["norm_mlp_g"],
             W["w_ff1"], W["w_ff2"], W["final_norm_g"], nb=nb, rows=rows)
    return (y, _pairs_to_state(s_pairs)[None], p_tail[None, :, 7, :],
            k_tail.reshape(1, b, n_tail, N_HEADS, HEAD_DIM),
            v_tail.reshape(1, b, n_tail, N_HEADS, HEAD_DIM))


def kernel(x_prompt, x_sample, c_prompt, c_sample, state_rwkv_wkv, state_rwkv_shift, cache_att_k, cache_att_v, w_ada, b_ada, norm_mix_g, w_in, mu_shift, w_lora_up, w0, a_lora_up, a0, g_lora_up, k_k, k_a, r_k, lnx_g, lnx_b, rel_table, w_br_r, w_br_a, w_out, norm_mlp_g, w_ff1, w_ff2, final_norm_g):
    bp, tp, d = x_prompt.shape
    bs, ts, _ = x_sample.shape
    l = 0
    row = lambda a: a.reshape(1, -1)
    zl = jnp.zeros((LORA_W, D_R), F32)
    w_wa = jnp.concatenate([jnp.concatenate([w_lora_up[l], zl], axis=1),
                            jnp.concatenate([zl, a_lora_up[l]], axis=1)], axis=0).astype(BF16)
    W = dict(
        norm_mix_g=norm_mix_g[l], w_in=w_in[l].astype(BF16),
        wkv=(row(mu_shift[l]), w_wa, row(w0[l]), row(a0[l]), g_lora_up[l].astype(BF16),
             row(k_k[l]), row(k_a[l]), row(r_k[l]), row(lnx_g[l]), row(lnx_b[l])),
        w_br_r=w_br_r[l].astype(BF16), w_br_a=w_br_a[l].astype(BF16), w_out=w_out[l].astype(BF16),
        norm_mlp_g=norm_mlp_g[l], w_ff1=w_ff1[l].astype(BF16), w_ff2=w_ff2[l].astype(BF16),
        final_norm_g=final_norm_g)

    mod = _ada(jnp.concatenate([c_prompt, c_sample], axis=0), w_ada[l], b_ada[l])
    mod_p = mod[:bp, None, :]
    mod_s = mod[bp:, None, :]

    bias_p = _rel_bias(rel_table[l], jnp.arange(CHUNK), jnp.arange(BAND) - WINDOW)
    attend_p = lambda q, k, v: _attn_prompt(q, k, v, bias_p, rows=WINDOW)
    n_keep = min(WINDOW, tp)
    y_p, p_wkv, p_shift, p_k, p_v = _group(
        x_prompt, mod_p, jnp.zeros((bp, 8, R_PROJ), F32),
        jnp.zeros((bp, N_HEADS, HEAD_DIM, HEAD_DIM), F32), attend_p, W,
        nb=1, rows=WINDOW, n_tail=n_keep)

    n_cache = cache_att_k.shape[2]
    q_pos = PAST_LEN + jnp.arange(ts)
    k_pos = jnp.concatenate([PAST_LEN - n_cache + jnp.arange(n_cache), q_pos])
    bias_s = _rel_bias(rel_table[l], q_pos, k_pos)
    ck = cache_att_k[l].reshape(bs, n_cache, D_A)
    cv = cache_att_v[l].reshape(bs, n_cache, D_A)
    attend_s = lambda q, k, v: _attn_sample(q, k, v, ck, cv, bias_s[:, :, :n_cache], bias_s[:, :, n_cache:])
    prev_s = jnp.broadcast_to(state_rwkv_shift[l][:, None, :], (bs, 8, R_PROJ))
    y_s, s_wkv, s_shift, s_k, s_v = _group(
        x_sample, mod_s, prev_s, state_rwkv_wkv[l], attend_s, W, nb=bs, rows=ts, n_tail=ts)

    return (y_p, y_s, p_wkv, p_shift, p_k, p_v, s_wkv, s_shift, s_k, s_v)
```

```python
import functools
import math

import jax
import jax.numpy as jnp
from jax import lax
from jax.experimental import pallas as pl
from jax.experimental.pallas import tpu as pltpu

D_MODEL = 1024
HEAD_DIM = 64
N_HEADS = 8
D_R = N_HEADS * HEAD_DIM
D_A = N_HEADS * HEAD_DIM
LORA_W = 64
LORA_A = 64
LORA_G = 128
R_PROJ = 3 * D_R + LORA_W + LORA_A + LORA_G
CHUNK = 64
BAND_CHUNKS = 8
WINDOW = BAND_CHUNKS * CHUNK
BAND = WINDOW + CHUNK
REL_CLIP = 128
D_FF = 4 * D_MODEL
IN_PROJ = R_PROJ + 3 * D_A + 2 * D_MODEL
NORM_EPS = 1e-6
GN_EPS = 64e-5
ATT_SCALE = HEAD_DIM ** -0.5
LOG2E = math.log2(math.e)
PAST_LEN = 1024

LANES = 128
N_PAIRS = N_HEADS // 2
NEG_BIG = -1e30
HEAD_SUM_TERMS = 1
WKV_GROUP_CHUNKS = 2
VMEM_LIMIT = 56 * 1024 * 1024

F32 = jnp.float32
BF16 = jnp.bfloat16


def _dot(a, b):
    return lax.dot_general(a, b, (((1,), (0,)), ((), ())), preferred_element_type=F32)


def _dot_nt(a, b):
    return lax.dot_general(a, b, (((1,), (1,)), ((), ())), preferred_element_type=F32)


def _dot_tn(a, b):
    return lax.dot_general(a, b, (((0,), (0,)), ((), ())), preferred_element_type=F32)


def _split_dot(x, w, terms):
    acc = None
    rem = x
    for _ in range(terms):
        piece = rem.astype(BF16)
        part = _dot(piece, w)
        acc = part if acc is None else acc + part
        rem = rem - piece.astype(F32)
    return acc


def _rms(x):
    return x * lax.rsqrt(jnp.mean(x * x, axis=-1, keepdims=True) + NORM_EPS)


def _const_spec(shape):
    nd = len(shape)
    return pl.BlockSpec(shape, lambda *_: (0,) * nd, pipeline_mode=pl.Buffered(1))


def _ada_kernel(c_ref, w_ref, b_ref, o_ref):
    c = c_ref[...]
    s = (c * jax.nn.sigmoid(c)).astype(BF16)
    o_ref[...] = _dot(s, w_ref[...].astype(BF16)) + b_ref[...]


def _ada(c, w_ada, b_ada):
    n, d = c.shape
    n_out = w_ada.shape[1]
    bn = 1024
    return pl.pallas_call(
        _ada_kernel,
        grid=(n_out // bn,),
        in_specs=[pl.BlockSpec((n, d), lambda j: (0, 0)),
                  pl.BlockSpec((d, bn), lambda j: (0, j)),
                  pl.BlockSpec((1, bn), lambda j: (0, j))],
        out_specs=pl.BlockSpec((n, bn), lambda j: (0, j)),
        out_shape=jax.ShapeDtypeStruct((n, n_out), F32),
        compiler_params=pltpu.CompilerParams(dimension_semantics=("arbitrary",)),
        name="ada",
    )(c, w_ada, b_ada.reshape(1, n_out))


def _proj_kernel(x_ref, mod_ref, g_ref, w_ref,
                 pr_ref, q_ref, k_ref, v_ref, gt_ref, ptail_ref, ktail_ref, vtail_ref, *, n_tail):
    nb, rows, d = x_ref.shape
    x = x_ref[...]
    mod = mod_ref[...]
    sh = mod[:, :, 0:d]
    sc = mod[:, :, d:2 * d]
    h = (_rms(x) * g_ref[...]) * (1.0 + sc) + sh
    h = h.reshape(nb * rows, d).astype(BF16)

    def col(lo, hi):
        return _dot(h, w_ref[:, lo:hi])

    last = pl.program_id(1) == pl.num_programs(1) - 1
    half = R_PROJ // 2
    for j in range(2):
        p = col(j * half, (j + 1) * half).reshape(nb, rows, half)
        pr_ref[:, :, j * half:(j + 1) * half] = p.astype(BF16)

        @pl.when(last)
        def _():
            ptail_ref[:, :, j * half:(j + 1) * half] = p[:, rows - 8:, :]

    q = col(R_PROJ, R_PROJ + D_A) * (ATT_SCALE * LOG2E)
    q_ref[...] = q.reshape(nb, rows, D_A).astype(BF16)
    for ref, tail, lo in ((k_ref, ktail_ref, R_PROJ + D_A), (v_ref, vtail_ref, R_PROJ + 2 * D_A)):
        t = col(lo, lo + D_A).reshape(nb, rows, D_A)
        ref[...] = t.astype(BF16)

        @pl.when(last)
        def _():
            tail[...] = t[:, rows - n_tail:, :]

    g0 = R_PROJ + 3 * D_A
    for j in range(2):
        gt = col(g0 + j * d, g0 + (j + 1) * d).reshape(nb, rows, d)
        gt_ref[:, :, j * d:(j + 1) * d] = gt.astype(BF16)


def _proj(x, mod, norm_g, w_in, *, nb, rows, n_tail):
    b, t, d = x.shape
    grid = (b // nb, t // rows)
    tile = lambda n: pl.BlockSpec((nb, rows, n), lambda i, j: (i, j, 0))
    tail = lambda r, n: pl.BlockSpec((nb, r, n), lambda i, j: (i, 0, 0))
    out_shape = (
        jax.ShapeDtypeStruct((b, t, R_PROJ), BF16),
        jax.ShapeDtypeStruct((b, t, D_A), BF16),
        jax.ShapeDtypeStruct((b, t, D_A), BF16),
        jax.ShapeDtypeStruct((b, t, D_A), BF16),
        jax.ShapeDtypeStruct((b, t, 2 * d), BF16),
        jax.ShapeDtypeStruct((b, 8, R_PROJ), F32),
        jax.ShapeDtypeStruct((b, n_tail, D_A), F32),
        jax.ShapeDtypeStruct((b, n_tail, D_A), F32),
    )
    return pl.pallas_call(
        functools.partial(_proj_kernel, n_tail=n_tail),
        grid=grid,
        in_specs=[tile(d),
                  pl.BlockSpec((nb, 1, 6 * d), lambda i, j: (i, 0, 0)),
                  _const_spec((1, 1, d)),
                  _const_spec((d, IN_PROJ))],
        out_specs=(tile(R_PROJ), tile(D_A), tile(D_A), tile(D_A), tile(2 * d),
                   tail(8, R_PROJ), tail(n_tail, D_A), tail(n_tail, D_A)),
        out_shape=out_shape,
        compiler_params=pltpu.CompilerParams(
            dimension_semantics=("arbitrary", "arbitrary"), vmem_limit_bytes=VMEM_LIMIT),
        name="proj",
    )(x, mod, norm_g.reshape(1, 1, d), w_in)


def _pair_blocks(x, m0):
    xb = x.astype(BF16)
    z = jnp.zeros_like(xb)
    return jnp.concatenate([jnp.where(m0, xb, z), jnp.where(m0, z, xb)], axis=0)


def _wkv_chunk_local(units, masks):
    m0, strict, incl, blockdiag = masks
    c = CHUNK
    zero = jnp.zeros((c, LANES), F32)
    zero2 = jnp.zeros((LANES, LANES), F32)
    pre = []
    for r, k, v, kk, b, lw, cum in units:
        r_t = r * jnp.exp(cum)
        kk_t = kk * jnp.exp(cum - lw)
        e_inv = jnp.exp(-cum)
        cum_last = cum[c - 1:c, :]
        dec = jnp.exp(cum_last - cum)
        pre.append(dict(
            r_t=r_t, kk_t=kk_t, v=v, lhs=jnp.concatenate([kk_t, r_t], axis=0).astype(BF16),
            k_h=_pair_blocks(k * e_inv, m0), b_h=_pair_blocks(b * e_inv, m0),
            v_blk=_pair_blocks(v, m0), k_p=k * dec, b_p=b * dec, p_last=jnp.exp(cum_last)))
    g1 = [_dot_nt(p["lhs"], p["k_h"]) for p in pre]
    g2 = [_dot_nt(p["lhs"], p["b_h"]) for p in pre]
    m_k = [jnp.where(strict, g[:c], zero).astype(BF16) for g in g1]
    a_rk = [jnp.where(incl, g[c:], zero) for g in g1]
    n_pow = [jnp.where(strict, g[:c], zero) for g in g2]
    a_rb = [jnp.where(incl, g[c:], zero) for g in g2]
    m_kv = [_dot(m, p["v_blk"]) for m, p in zip(m_k, pre)]

    def blocks2(y):
        return jnp.concatenate([_pair_blocks(y[:, :LANES], m0), _pair_blocks(y[:, LANES:], m0)], axis=1)

    y = [jnp.concatenate([mv, p["kk_t"]], axis=1) for mv, p in zip(m_kv, pre)]
    y = [yi - _dot(n.astype(BF16), blocks2(yi)) for yi, n in zip(y, n_pow)]
    for _ in range(int(math.log2(c)) - 1):
        n_pow = [_dot(n.astype(BF16), _pair_blocks(n, m0)) for n in n_pow]
        y = [yi + _dot(n.astype(BF16), blocks2(yi)) for yi, n in zip(y, n_pow)]
    u_loc = [yi[:, :LANES] for yi in y]
    q = [yi[:, LANES:] for yi in y]

    o_loc = [_dot(jnp.concatenate([ak, ab], axis=1).astype(BF16),
                  jnp.concatenate([p["v_blk"], _pair_blocks(-u, m0)], axis=0))
             for ak, ab, u, p in zip(a_rk, a_rb, u_loc, pre)]
    r_eff = [p["r_t"] - _dot(ab.astype(BF16), _pair_blocks(qi, m0)) for p, ab, qi in zip(pre, a_rb, q)]
    d_s = [jnp.where(blockdiag,
                     _dot_tn(jnp.concatenate([p["v"], u], axis=0).astype(BF16),
                             jnp.concatenate([p["k_p"], -p["b_p"]], axis=0).astype(BF16)), zero2)
           for p, u in zip(pre, u_loc)]
    qtb = [jnp.where(blockdiag, _dot_tn(qi.astype(BF16), p["b_p"].astype(BF16)), zero2)
           for p, qi in zip(pre, q)]
    return [(re, ol, qb, ds, p["p_last"]) for re, ol, qb, ds, p in zip(r_eff, o_loc, qtb, d_s, pre)]


def _wkv_kernel(p_ref, prev0_ref, s0_ref, mu_ref, wwa_ref, w0_ref, a0_ref, gup_ref,
                kk_ref, ka_ref, rk_ref, lng_ref, lnb_ref,
                y_ref, sout_ref,
                s_scr, prev_scr, r_scr, k_scr, v_scr, kks_scr, b_scr, lw_scr, o_scr,
                reff_scr, qtb_scr, ds_scr, pc_scr):
    rows = p_ref.shape[1]
    rows_pad = r_scr.shape[0]
    n_chunks = rows_pad // CHUNK
    t_idx = pl.program_id(1)

    @pl.when(t_idx == 0)
    def _():
        s_scr[...] = s0_ref[0]
        prev_scr[...] = prev0_ref[0]

    p = p_ref[0].astype(F32)
    row = lax.broadcasted_iota(jnp.int32, (rows, 1), 0)
    p_prev = jnp.where(row == 0, prev_scr[0:1, :], pltpu.roll(p, 1, axis=0))
    prev_scr[0:1, :] = p[rows - 1:rows, :]
    xs = p + (p_prev - p) * mu_ref[...]

    r = xs[:, 0:D_R]
    k = xs[:, D_R:2 * D_R]
    v = xs[:, 2 * D_R:3 * D_R]
    x_wa = xs[:, 3 * D_R:3 * D_R + LANES]
    x_g = xs[:, 3 * D_R + LANES:R_PROJ]

    lane = lax.broadcasted_iota(jnp.int32, (rows, LANES), 1)
    wa_in = jnp.where(lane < LORA_W, jnp.tanh(x_wa), x_wa).astype(BF16)
    wa = _dot(wa_in, wwa_ref[...])
    lw = -math.exp(-0.5) * jax.nn.sigmoid(wa[:, 0:D_R] + w0_ref[...])
    a = jax.nn.sigmoid(wa[:, D_R:2 * D_R] + a0_ref[...])
    g = _dot(jax.nn.sigmoid(x_g).astype(BF16), gup_ref[...])

    hr = lax.broadcasted_iota(jnp.int32, (D_R, D_R), 0) // HEAD_DIM
    hc = lax.broadcasted_iota(jnp.int32, (D_R, D_R), 1) // HEAD_DIM
    head_ones = jnp.where(hr == hc, 1.0, 0.0).astype(BF16)

    kk = k * kk_ref[...]
    kk = kk * lax.rsqrt(jnp.maximum(_split_dot(kk * kk, head_ones, HEAD_SUM_TERMS), 1e-24))
    k2 = k * (1.0 + (a - 1.0) * ka_ref[...])
    b = a * kk

    def put(ref, val):
        ref[0:rows, :] = val
        if rows_pad > rows:
            ref[rows:rows_pad, :] = jnp.zeros((rows_pad - rows, D_R), F32)

    put(r_scr, r)
    put(k_scr, k2)
    put(v_scr, v)
    put(kks_scr, kk)
    put(b_scr, b)
    put(lw_scr, lw)

    ci = lax.broadcasted_iota(jnp.int32, (CHUNK, CHUNK), 0)
    cj = lax.broadcasted_iota(jnp.int32, (CHUNK, CHUNK), 1)
    tri = jnp.where(cj <= ci, 1.0, 0.0).astype(BF16)
    lane_c = lax.broadcasted_iota(jnp.int32, (CHUNK, LANES), 1)
    row_c = lax.broadcasted_iota(jnp.int32, (CHUNK, LANES), 0)
    m0 = lane_c < HEAD_DIM
    jj = jnp.where(m0, lane_c, lane_c - HEAD_DIM)
    br = lax.broadcasted_iota(jnp.int32, (LANES, LANES), 0) < HEAD_DIM
    bc = lax.broadcasted_iota(jnp.int32, (LANES, LANES), 1) < HEAD_DIM
    masks = (m0, jj < row_c, jj <= row_c, br == bc)

    group = math.gcd(n_chunks, WKV_GROUP_CHUNKS)

    def local_body(gi, carry):
        units = []
        for cc in range(group):
            r0 = pl.multiple_of((gi * group + cc) * CHUNK, CHUNK)
            sl = pl.ds(r0, CHUNK)
            lw_c = lw_scr[sl, :]
            cum = _split_dot_lhs(tri, lw_c)
            for pr in range(N_PAIRS):
                ls = slice(pr * LANES, (pr + 1) * LANES)
                units.append((r_scr[sl, ls], k_scr[sl, ls], v_scr[sl, ls], kks_scr[sl, ls],
                              b_scr[sl, ls], lw_c[:, ls], cum[:, ls]))
        res = _wkv_chunk_local(units, masks)
        for cc in range(group):
            ci_ = gi * group + cc
            sl = pl.ds(pl.multiple_of(ci_ * CHUNK, CHUNK), CHUNK)
            for pr in range(N_PAIRS):
                r_eff, o_loc, qtb, d_s, p_last = res[cc * N_PAIRS + pr]
                u = ci_ * N_PAIRS + pr
                reff_scr[u] = r_eff.astype(BF16)
                o_scr[sl, pr * LANES:(pr + 1) * LANES] = o_loc
                qtb_scr[u] = qtb.astype(BF16)
                ds_scr[u] = d_s
                pc_scr[u] = jnp.broadcast_to(p_last, (8, LANES))
        return carry

    lax.fori_loop(0, n_chunks // group, local_body, 0)

    def state_body(ci_, carry):
        sl = pl.ds(pl.multiple_of(ci_ * CHUNK, CHUNK), CHUNK)
        s_old = [s_scr[pr] for pr in range(N_PAIRS)]
        s_bf = [s.astype(BF16) for s in s_old]
        us = [ci_ * N_PAIRS + pr for pr in range(N_PAIRS)]
        o_add = [_dot_nt(reff_scr[u], sb) for u, sb in zip(us, s_bf)]
        s_mix = [_dot(sb, qtb_scr[u]) for u, sb in zip(us, s_bf)]
        for pr in range(N_PAIRS):
            ls = slice(pr * LANES, (pr + 1) * LANES)
            o_scr[sl, ls] = o_scr[sl, ls] + o_add[pr]
            s_scr[pr] = s_old[pr] * pc_scr[us[pr]][0:1, :] + ds_scr[us[pr]] - s_mix[pr]
        return carry

    lax.fori_loop(0, n_chunks, state_body, 0)

    o = o_scr[0:rows, :]
    inv_n = 1.0 / HEAD_DIM
    mean = _split_dot(o, head_ones, HEAD_SUM_TERMS) * inv_n
    cen = o - mean
    var = _split_dot(cen * cen, head_ones, HEAD_SUM_TERMS) * inv_n
    o_n = cen * lax.rsqrt(var + GN_EPS) * lng_ref[...] + lnb_ref[...]
    bonus = _split_dot(r * k2 * rk_ref[...], head_ones, HEAD_SUM_TERMS) * v
    y_ref[0] = ((o_n + bonus) * g).astype(BF16)

    @pl.when(t_idx == pl.num_programs(1) - 1)
    def _():
        sout_ref[0] = s_scr[...]


def _split_dot_lhs(tri, x):
    hi = x.astype(BF16)
    r1 = x - hi.astype(F32)
    mid = r1.astype(BF16)
    lo = (r1 - mid.astype(F32)).astype(BF16)
    return _dot(tri, hi) + _dot(tri, mid) + _dot(tri, lo)


def _wkv(p_r, prev0, s0_pairs, wp, *, rows):
    b, t, _ = p_r.shape
    rows_pad = -(-rows // CHUNK) * CHUNK
    n_units = rows_pad // CHUNK * N_PAIRS
    grid = (b, t // rows)
    vec = lambda n: _const_spec((1, n))
    scr = lambda: pltpu.VMEM((rows_pad, D_R), F32)
    return pl.pallas_call(
        _wkv_kernel,
        grid=grid,
        in_specs=[pl.BlockSpec((1, rows, R_PROJ), lambda i, j: (i, j, 0)),
                  pl.BlockSpec((1, 8, R_PROJ), lambda i, j: (i, 0, 0)),
                  pl.BlockSpec((1, N_PAIRS, LANES, LANES), lambda i, j: (i, 0, 0, 0)),
                  vec(R_PROJ),
                  _const_spec((LANES, 2 * D_R)),
                  vec(D_R), vec(D_R),
                  _const_spec((LORA_G, D_R)),
                  vec(D_R), vec(D_R), vec(D_R), vec(D_R), vec(D_R)],
        out_specs=(pl.BlockSpec((1, rows, D_R), lambda i, j: (i, j, 0)),
                   pl.BlockSpec((1, N_PAIRS, LANES, LANES), lambda i, j: (i, 0, 0, 0))),
        out_shape=(jax.ShapeDtypeStruct((b, t, D_R), BF16),
                   jax.ShapeDtypeStruct((b, N_PAIRS, LANES, LANES), F32)),
        scratch_shapes=[pltpu.VMEM((N_PAIRS, LANES, LANES), F32),
                        pltpu.VMEM((8, R_PROJ), F32),
                        scr(), scr(), scr(), scr(), scr(), scr(), scr(),
                        pltpu.VMEM((n_units, CHUNK, LANES), BF16),
                        pltpu.VMEM((n_units, LANES, LANES), BF16),
                        pltpu.VMEM((n_units, LANES, LANES), F32),
                        pltpu.VMEM((n_units, 8, LANES), F32)],
        compiler_params=pltpu.CompilerParams(
            dimension_semantics=("arbitrary", "arbitrary"), vmem_limit_bytes=VMEM_LIMIT),
        name="wkv",
    )(p_r, prev0, s0_pairs, *wp)


def _state_to_pairs(s):
    b = s.shape[0]
    s = s.reshape(b, N_PAIRS, 2, HEAD_DIM, HEAD_DIM)
    z = jnp.zeros_like(s[:, :, 0])
    top = jnp.concatenate([s[:, :, 0], z], axis=-1)
    bot = jnp.concatenate([z, s[:, :, 1]], axis=-1)
    return jnp.concatenate([top, bot], axis=-2)


def _pairs_to_state(sp):
    b = sp.shape[0]
    h0 = sp[:, :, :HEAD_DIM, :HEAD_DIM]
    h1 = sp[:, :, HEAD_DIM:, HEAD_DIM:]
    return jnp.stack([h0, h1], axis=2).reshape(b, N_HEADS, HEAD_DIM, HEAD_DIM)


def _softmax_pv(units, m0):
    mx = []
    for parts in units:
        m = None
        for s, _ in parts:
            mi = jnp.max(s, axis=-1, keepdims=True)
            m = mi if m is None else jnp.maximum(m, mi)
        mx.append(m)
    es = [[jnp.exp2(s - m) for s, _ in parts] for parts, m in zip(units, mx)]
    ls = [functools.reduce(lambda a, b: a + b, [jnp.sum(e, axis=-1, keepdims=True) for e in ep])
          for ep in es]
    pvs = [functools.reduce(lambda a, b: a + b,
                            [_dot(e.astype(BF16), vals) for e, (_, vals) in zip(ep, parts)])
           for ep, parts in zip(es, units)]
    out = []
    for pv, l in zip(pvs, ls):
        pv = pv / l
        c = pv.shape[0] // 2
        out.append(jnp.where(m0, pv[:c], pv[c:]))
    return out


def _pair_queries(q, m0):
    z = jnp.zeros_like(q)
    return jnp.concatenate([jnp.where(m0, q, z), jnp.where(m0, z, q)], axis=0)


def _attn_prompt_kernel(q_ref, kc_ref, kp_ref, vc_ref, vp_ref, bias_ref, y_ref, k_scr, v_scr):
    rows = q_ref.shape[1]
    n_chunks = rows // CHUNK
    t_idx = pl.program_id(1)
    k_scr[0:WINDOW, :] = kp_ref[0]
    k_scr[WINDOW:WINDOW + rows, :] = kc_ref[0]
    v_scr[0:WINDOW, :] = vp_ref[0]
    v_scr[WINDOW:WINDOW + rows, :] = vc_ref[0]
    m0 = lax.broadcasted_iota(jnp.int32, (CHUNK, LANES), 1) < HEAD_DIM
    key_i = lax.broadcasted_iota(jnp.int32, (1, BAND), 1)
    ri = lax.broadcasted_iota(jnp.int32, (2 * CHUNK, 2 * CHUNK), 0)
    rj = lax.broadcasted_iota(jnp.int32, (2 * CHUNK, 2 * CHUNK), 1)
    eye = jnp.where(ri == rj, 1.0, 0.0).astype(BF16)

    def chunk_body(c, carry, *, masked):
        r0 = pl.multiple_of(c * CHUNK, CHUNK)
        band = pl.ds(r0, BAND)
        units = []
        for pr in range(N_PAIRS):
            ls = slice(pr * LANES, (pr + 1) * LANES)
            lhs = jnp.concatenate([_pair_queries(q_ref[0, pl.ds(r0, CHUNK), ls], m0), eye], axis=1)
            rhs = jnp.concatenate([k_scr[band, ls], bias_ref[pr]], axis=1)
            s = _dot_nt(lhs, rhs)
            if masked:
                s = s + jnp.where(key_i + (r0 - WINDOW) >= 0, 0.0, NEG_BIG)
            units.append([(s, v_scr[band, ls])])
        for pr, o in enumerate(_softmax_pv(units, m0)):
            y_ref[0, pl.ds(r0, CHUNK), pr * LANES:(pr + 1) * LANES] = o.astype(BF16)
        return carry

    @pl.when(t_idx == 0)
    def _():
        lax.fori_loop(0, n_chunks, functools.partial(chunk_body, masked=True), 0)

    @pl.when(t_idx > 0)
    def _():
        lax.fori_loop(0, n_chunks, functools.partial(chunk_body, masked=False), 0)


def _attn_prompt(q, k, v, bias_pairs, *, rows):
    b, t, _ = q.shape
    assert rows == WINDOW
    cur = pl.BlockSpec((1, rows, D_A), lambda i, j: (i, j, 0))
    prev = pl.BlockSpec((1, rows, D_A), lambda i, j: (i, jnp.maximum(j - 1, 0), 0))
    return pl.pallas_call(
        _attn_prompt_kernel,
        grid=(b, t // rows),
        in_specs=[cur, cur, prev, cur, prev, _const_spec((N_PAIRS, BAND, LANES))],
        out_specs=cur,
        out_shape=jax.ShapeDtypeStruct((b, t, D_A), BF16),
        scratch_shapes=[pltpu.VMEM((WINDOW + rows, D_A), BF16),
                        pltpu.VMEM((WINDOW + rows, D_A), BF16)],
        compiler_params=pltpu.CompilerParams(
            dimension_semantics=("arbitrary", "arbitrary"), vmem_limit_bytes=VMEM_LIMIT),
        name="attn_prompt",
    )(q, k, k, v, v, bias_pairs)


def _attn_sample_kernel(q_ref, k_ref, v_ref, ck_ref, cv_ref, bias_c_ref, bias_n_ref, y_ref):
    rows = q_ref.shape[1]
    m0 = lax.broadcasted_iota(jnp.int32, (rows, LANES), 1) < HEAD_DIM
    units = []
    for pr in range(N_PAIRS):
        ls = slice(pr * LANES, (pr + 1) * LANES)
        lhs = _pair_queries(q_ref[0, :, ls], m0)
        s_c = _dot_nt(lhs, ck_ref[0, :, ls].astype(BF16)) + bias_c_ref[pr]
        s_n = _dot_nt(lhs, k_ref[0, :, ls]) + bias_n_ref[pr]
        units.append([(s_c, cv_ref[0, :, ls].astype(BF16)), (s_n, v_ref[0, :, ls])])
    for pr, o in enumerate(_softmax_pv(units, m0)):
        y_ref[0, :, pr * LANES:(pr + 1) * LANES] = o.astype(BF16)


def _attn_sample(q, k, v, cache_k, cache_v, bias_c, bias_n):
    b, rows, _ = q.shape
    n_cache = cache_k.shape[1]
    cur = pl.BlockSpec((1, rows, D_A), lambda i: (i, 0, 0))
    cache = pl.BlockSpec((1, n_cache, D_A), lambda i: (i, 0, 0))
    return pl.pallas_call(
        _attn_sample_kernel,
        grid=(b,),
        in_specs=[cur, cur, cur, cache, cache,
                  _const_spec((N_PAIRS, 2 * rows, n_cache)), _const_spec((N_PAIRS, 2 * rows, rows))],
        out_specs=cur,
        out_shape=jax.ShapeDtypeStruct((b, rows, D_A), BF16),
        compiler_params=pltpu.CompilerParams(dimension_semantics=("arbitrary",)),
        name="attn_sample",
    )(q, k, v, cache_k, cache_v, bias_c, bias_n)


def _rel_bias(table, n_q, n_k, offset):
    h = table.shape[0]
    period = n_q + n_k
    e = jnp.arange(period)
    e = jnp.where(e < n_k, e, e - period)
    u = table[:, jnp.clip(offset - e, -REL_CLIP, REL_CLIP) + REL_CLIP].astype(F32) * LOG2E
    toeplitz = jnp.tile(u, (1, n_q))[:, :n_q * (period - 1)].reshape(h, n_q, period - 1)
    return toeplitz[:, :, :n_k]


def _pair_rows(bias):
    h, nq, nk = bias.shape
    return bias.reshape(h // 2, 2 * nq, nk)


def _out_kernel(x_ref, yr_ref, ya_ref, gt_ref, mod_ref, wbr_ref, wba_ref, wo_ref, ng_ref,
                w1_ref, w2_ref, fg_ref, y_ref):
    nb, rows, d = x_ref.shape
    n = nb * rows
    mod = mod_ref[...]
    g1 = mod[:, :, 2 * d:3 * d]
    sh2 = mod[:, :, 3 * d:4 * d]
    sc2 = mod[:, :, 4 * d:5 * d]
    g2 = mod[:, :, 5 * d:6 * d]

    gates = gt_ref[...].reshape(n, 2 * d).astype(F32)
    y_r = yr_ref[...].reshape(n, D_R)
    y_a = ya_ref[...].reshape(n, D_A)
    merged = (jax.nn.sigmoid(gates[:, 0:d]) * _dot(y_r, wbr_ref[...])
              + jax.nn.sigmoid(gates[:, d:2 * d]) * _dot(y_a, wba_ref[...]))
    mix = _dot(merged.astype(BF16), wo_ref[...]).reshape(nb, rows, d)
    x1 = x_ref[...] + g1 * mix

    h2 = ((_rms(x1) * ng_ref[...]) * (1.0 + sc2) + sh2).reshape(n, d).astype(BF16)
    ff_blk = 1024
    acc = None
    for j in range(D_FF // ff_blk):
        mid = _dot(h2, w1_ref[:, j * ff_blk:(j + 1) * ff_blk])
        act = jnp.square(jnp.maximum(mid, 0.0)).astype(BF16)
        part = _dot(act, w2_ref[j * ff_blk:(j + 1) * ff_blk, :])
        acc = part if acc is None else acc + part
    x2 = x1 + g2 * acc.reshape(nb, rows, d)
    y_ref[...] = _rms(x2) * fg_ref[...]


def _out(x, y_r, y_a, gates, mod, w_br_r, w_br_a, w_out, norm_g, w_ff1, w_ff2, final_g, *, nb, rows):
    b, t, d = x.shape
    tile = lambda n: pl.BlockSpec((nb, rows, n), lambda i, j: (i, j, 0))
    return pl.pallas_call(
        _out_kernel,
        grid=(b // nb, t // rows),
        in_specs=[tile(d), tile(D_R), tile(D_A), tile(2 * d),
                  pl.BlockSpec((nb, 1, 6 * d), lambda i, j: (i, 0, 0)),
                  _const_spec((D_R, d)), _const_spec((D_A, d)), _const_spec((d, d)),
                  _const_spec((1, 1, d)),
                  _const_spec((d, D_FF)), _const_spec((D_FF, d)),
                  _const_spec((1, 1, d))],
        out_specs=tile(d),
        out_shape=jax.ShapeDtypeStruct((b, t, d), F32),
        compiler_params=pltpu.CompilerParams(
            dimension_semantics=("arbitrary", "arbitrary"), vmem_limit_bytes=VMEM_LIMIT),
        name="out",
    )(x, y_r, y_a, gates, mod, w_br_r, w_br_a, w_out, norm_g.reshape(1, 1, d),
      w_ff1, w_ff2, final_g.reshape(1, 1, d))


def _group(x, mod, prev0, s0, attend, W, *, nb, rows, n_tail):
    b, t, d = x.shape
    p_r, q, k, v, gates, p_tail, k_tail, v_tail = _proj(
        x, mod, W["norm_mix_g"], W["w_in"], nb=nb, rows=rows, n_tail=n_tail)
    y_r, s_pairs = _wkv(p_r, prev0, _state_to_pairs(s0), W["wkv"], rows=rows)
    y_a = attend(q, k, v)
    y = _out(x, y_r, y_a, gates, mod, W["w_br_r"], W["w_br_a"], W["w_out"], W["norm_mlp_g"],
             W["w_ff1"], W["w_ff2"], W["final_norm_g"], nb=nb, rows=rows)
    return (y, _pairs_to_state(s_pairs)[None], p_tail[None, :, 7, :],
            k_tail.reshape(1, b, n_tail, N_HEADS, HEAD_DIM),
            v_tail.reshape(1, b, n_tail, N_HEADS, HEAD_DIM))


def kernel(x_prompt, x_sample, c_prompt, c_sample, state_rwkv_wkv, state_rwkv_shift, cache_att_k, cache_att_v, w_ada, b_ada, norm_mix_g, w_in, mu_shift, w_lora_up, w0, a_lora_up, a0, g_lora_up, k_k, k_a, r_k, lnx_g, lnx_b, rel_table, w_br_r, w_br_a, w_out, norm_mlp_g, w_ff1, w_ff2, final_norm_g):
    bp, tp, d = x_prompt.shape
    bs, ts, _ = x_sample.shape
    l = 0
    row = lambda a: a.reshape(1, -1)
    zl = jnp.zeros((LORA_W, D_R), F32)
    w_wa = jnp.concatenate([jnp.concatenate([w_lora_up[l], zl], axis=1),
                            jnp.concatenate([zl, a_lora_up[l]], axis=1)], axis=0).astype(BF16)
    W = dict(
        norm_mix_g=norm_mix_g[l], w_in=w_in[l].astype(BF16),
        wkv=(row(mu_shift[l]), w_wa, row(w0[l]), row(a0[l]), g_lora_up[l].astype(BF16),
             row(k_k[l]), row(k_a[l]), row(r_k[l]), row(lnx_g[l]), row(lnx_b[l])),
        w_br_r=w_br_r[l].astype(BF16), w_br_a=w_br_a[l].astype(BF16), w_out=w_out[l].astype(BF16),
        norm_mlp_g=norm_mlp_g[l], w_ff1=w_ff1[l].astype(BF16), w_ff2=w_ff2[l].astype(BF16),
        final_norm_g=final_norm_g)

    mod = _ada(jnp.concatenate([c_prompt, c_sample], axis=0), w_ada[l], b_ada[l])
    mod_p = mod[:bp, None, :]
    mod_s = mod[bp:, None, :]

    bias_p = _rel_bias(rel_table[l], CHUNK, BAND, WINDOW)
    bias_t = bias_p.reshape(N_PAIRS, 2 * CHUNK, BAND).transpose(0, 2, 1).astype(BF16)
    attend_p = lambda q, k, v: _attn_prompt(q, k, v, bias_t, rows=WINDOW)
    n_keep = min(WINDOW, tp)
    y_p, p_wkv, p_shift, p_k, p_v = _group(
        x_prompt, mod_p, jnp.zeros((bp, 8, R_PROJ), F32),
        jnp.zeros((bp, N_HEADS, HEAD_DIM, HEAD_DIM), F32), attend_p, W,
        nb=1, rows=WINDOW, n_tail=n_keep)

    n_cache = cache_att_k.shape[2]
    bias_s = _pair_rows(_rel_bias(rel_table[l], ts, n_cache + ts, n_cache))
    ck = cache_att_k[l].reshape(bs, n_cache, D_A)
    cv = cache_att_v[l].reshape(bs, n_cache, D_A)
    attend_s = lambda q, k, v: _attn_sample(q, k, v, ck, cv, bias_s[:, :, :n_cache], bias_s[:, :, n_cache:])
    prev_s = jnp.broadcast_to(state_rwkv_shift[l][:, None, :], (bs, 8, R_PROJ))
    y_s, s_wkv, s_shift, s_k, s_v = _group(
        x_sample, mod_s, prev_s, state_rwkv_wkv[l], attend_s, W, nb=bs, rows=ts, n_tail=ts)

    return (y_p, y_s, p_wkv, p_shift, p_k, p_v, s_wkv, s_shift, s_k, s_v)
```

```python
import functools
import math

import jax
import jax.numpy as jnp
from jax import lax
from jax.experimental import pallas as pl
from jax.experimental.pallas import tpu as pltpu

D_MODEL = 1024
HEAD_DIM = 64
N_HEADS = 8
D_R = N_HEADS * HEAD_DIM
D_A = N_HEADS * HEAD_DIM
LORA_W = 64
LORA_A = 64
LORA_G = 128
R_PROJ = 3 * D_R + LORA_W + LORA_A + LORA_G
CHUNK = 64
BAND_CHUNKS = 8
WINDOW = BAND_CHUNKS * CHUNK
BAND = WINDOW + CHUNK
REL_CLIP = 128
D_FF = 4 * D_MODEL
IN_PROJ = R_PROJ + 3 * D_A + 2 * D_MODEL
NORM_EPS = 1e-6
GN_EPS = 64e-5
ATT_SCALE = HEAD_DIM ** -0.5
LOG2E = math.log2(math.e)
PAST_LEN = 1024

LANES = 128
N_PAIRS = N_HEADS // 2
NEG_BIG = -1e30
HEAD_SUM_TERMS = 1
WKV_GROUP_CHUNKS = 4
VMEM_LIMIT = 56 * 1024 * 1024

F32 = jnp.float32
BF16 = jnp.bfloat16


def _dot(a, b):
    return lax.dot_general(a, b, (((1,), (0,)), ((), ())), preferred_element_type=F32)


def _dot_nt(a, b):
    return lax.dot_general(a, b, (((1,), (1,)), ((), ())), preferred_element_type=F32)


def _dot_tn(a, b):
    return lax.dot_general(a, b, (((0,), (0,)), ((), ())), preferred_element_type=F32)


def _split_dot(x, w, terms):
    acc = None
    rem = x
    for _ in range(terms):
        piece = rem.astype(BF16)
        part = _dot(piece, w)
        acc = part if acc is None else acc + part
        rem = rem - piece.astype(F32)
    return acc


def _sigmoid(x):
    return 0.5 * jnp.tanh(0.5 * x) + 0.5


def _rms(x):
    return x * lax.rsqrt(jnp.mean(x * x, axis=-1, keepdims=True) + NORM_EPS)


def _const_spec(shape):
    nd = len(shape)
    return pl.BlockSpec(shape, lambda *_: (0,) * nd, pipeline_mode=pl.Buffered(1))


def _ada_kernel(c_ref, w_ref, b_ref, o_ref):
    c = c_ref[...]
    s = (c * jax.nn.sigmoid(c)).astype(BF16)
    o_ref[...] = _dot(s, w_ref[...].astype(BF16)) + b_ref[...]


def _ada(c, w_ada, b_ada):
    n, d = c.shape
    n_out = w_ada.shape[1]
    bn = 1024
    return pl.pallas_call(
        _ada_kernel,
        grid=(n_out // bn,),
        in_specs=[pl.BlockSpec((n, d), lambda j: (0, 0)),
                  pl.BlockSpec((d, bn), lambda j: (0, j)),
                  pl.BlockSpec((1, bn), lambda j: (0, j))],
        out_specs=pl.BlockSpec((n, bn), lambda j: (0, j)),
        out_shape=jax.ShapeDtypeStruct((n, n_out), F32),
        compiler_params=pltpu.CompilerParams(dimension_semantics=("arbitrary",)),
        name="ada",
    )(c, w_ada, b_ada.reshape(1, n_out))


def _proj_kernel(x_ref, mod_ref, g_ref, w_ref, prev0_ref, mu_ref,
                 xs_ref, q_ref, k_ref, v_ref, gt_ref, ptail_ref, ktail_ref, vtail_ref,
                 prev_scr, *, n_tail):
    nb, rows, d = x_ref.shape
    x = x_ref[...]
    mod = mod_ref[...]
    sh = mod[:, :, 0:d]
    sc = mod[:, :, d:2 * d]
    h = (_rms(x) * g_ref[...]) * (1.0 + sc) + sh
    h = h.reshape(nb * rows, d).astype(BF16)

    def col(lo, hi):
        return _dot(h, w_ref[:, lo:hi])

    @pl.when(pl.program_id(1) == 0)
    def _():
        prev_scr[...] = prev0_ref[...]

    last = pl.program_id(1) == pl.num_programs(1) - 1
    first_row = lax.broadcasted_iota(jnp.int32, (1, rows, 1), 1) == 0
    half = R_PROJ // 2
    for j in range(2):
        cs = slice(j * half, (j + 1) * half)
        p2 = col(j * half, (j + 1) * half)
        p = p2.reshape(nb, rows, half)
        shifted = pltpu.roll(p2, 1, axis=0).reshape(nb, rows, half)
        p_prev = jnp.where(first_row, prev_scr[:, 0:1, cs], shifted)
        prev_scr[:, 0:1, cs] = p[:, rows - 1:rows, :]
        xs_ref[:, :, cs] = (p + (p_prev - p) * mu_ref[:, :, cs]).astype(BF16)

        @pl.when(last)
        def _():
            ptail_ref[:, :, cs] = p[:, rows - 8:, :]

    q = col(R_PROJ, R_PROJ + D_A) * (ATT_SCALE * LOG2E)
    q_ref[...] = q.reshape(nb, rows, D_A).astype(BF16)
    for ref, tail, lo in ((k_ref, ktail_ref, R_PROJ + D_A), (v_ref, vtail_ref, R_PROJ + 2 * D_A)):
        t = col(lo, lo + D_A).reshape(nb, rows, D_A)
        ref[...] = t.astype(BF16)

        @pl.when(last)
        def _():
            tail[...] = t[:, rows - n_tail:, :]

    g0 = R_PROJ + 3 * D_A
    for j in range(2):
        gt = col(g0 + j * d, g0 + (j + 1) * d).reshape(nb, rows, d)
        gt_ref[:, :, j * d:(j + 1) * d] = gt.astype(BF16)


def _proj(x, mod, norm_g, w_in, prev0, mu, *, nb, rows, n_tail):
    b, t, d = x.shape
    grid = (b // nb, t // rows)
    tile = lambda n: pl.BlockSpec((nb, rows, n), lambda i, j: (i, j, 0))
    tail = lambda r, n: pl.BlockSpec((nb, r, n), lambda i, j: (i, 0, 0))
    out_shape = (
        jax.ShapeDtypeStruct((b, t, R_PROJ), BF16),
        jax.ShapeDtypeStruct((b, t, D_A), BF16),
        jax.ShapeDtypeStruct((b, t, D_A), BF16),
        jax.ShapeDtypeStruct((b, t, D_A), BF16),
        jax.ShapeDtypeStruct((b, t, 2 * d), BF16),
        jax.ShapeDtypeStruct((b, 8, R_PROJ), F32),
        jax.ShapeDtypeStruct((b, n_tail, D_A), F32),
        jax.ShapeDtypeStruct((b, n_tail, D_A), F32),
    )
    return pl.pallas_call(
        functools.partial(_proj_kernel, n_tail=n_tail),
        grid=grid,
        in_specs=[tile(d),
                  pl.BlockSpec((nb, 1, 6 * d), lambda i, j: (i, 0, 0)),
                  _const_spec((1, 1, d)),
                  _const_spec((d, IN_PROJ)),
                  tail(8, R_PROJ),
                  _const_spec((1, 1, R_PROJ))],
        out_specs=(tile(R_PROJ), tile(D_A), tile(D_A), tile(D_A), tile(2 * d),
                   tail(8, R_PROJ), tail(n_tail, D_A), tail(n_tail, D_A)),
        out_shape=out_shape,
        scratch_shapes=[pltpu.VMEM((nb, 8, R_PROJ), F32)],
        compiler_params=pltpu.CompilerParams(
            dimension_semantics=("arbitrary", "arbitrary"), vmem_limit_bytes=VMEM_LIMIT),
        name="proj",
    )(x, mod, norm_g.reshape(1, 1, d), w_in, prev0, mu.reshape(1, 1, R_PROJ))


def _pair_blocks(x, m0):
    xb = x.astype(BF16)
    z = jnp.zeros_like(xb)
    return jnp.concatenate([jnp.where(m0, xb, z), jnp.where(m0, z, xb)], axis=0)


def _wkv_chunk_local(units, masks):
    m0, strict, incl, blockdiag = masks
    c = CHUNK
    eye = jnp.where(incl, 1.0, 0.0) - jnp.where(strict, 1.0, 0.0)
    zero = jnp.zeros((c, LANES), F32)
    zero2 = jnp.zeros((LANES, LANES), F32)
    pre = []
    for r, k, v, kk, b, lw, cum in units:
        r_t = r * jnp.exp(cum)
        kk_t = kk * jnp.exp(cum - lw)
        e_inv = jnp.exp(-cum)
        cum_last = cum[c - 1:c, :]
        dec = jnp.exp(cum_last - cum)
        pre.append(dict(
            r_t=r_t, kk_t=kk_t, v=v, lhs=jnp.concatenate([kk_t, r_t], axis=0).astype(BF16),
            k_h=_pair_blocks(k * e_inv, m0), b_h=_pair_blocks(b * e_inv, m0),
            v_blk=_pair_blocks(v, m0), k_p=k * dec, b_p=b * dec, p_last=jnp.exp(cum_last)))
    g1 = [_dot_nt(p["lhs"], p["k_h"]) for p in pre]
    g2 = [_dot_nt(p["lhs"], p["b_h"]) for p in pre]
    m_k = [jnp.where(strict, g[:c], zero).astype(BF16) for g in g1]
    a_rk = [jnp.where(incl, g[c:], zero) for g in g1]
    n_pow = [jnp.where(strict, g[:c], zero) for g in g2]
    a_rb = [jnp.where(incl, g[c:], zero) for g in g2]
    m_kv = [_dot(m, p["v_blk"]) for m, p in zip(m_k, pre)]

    t_inv = [eye - n for n in n_pow]
    n_pow = [_dot(n.astype(BF16), _pair_blocks(n, m0)) for n in n_pow]
    n_levels = int(math.log2(c)) - 1
    for level in range(n_levels):
        n_blk = [_pair_blocks(n, m0) for n in n_pow]
        if level == n_levels - 1:
            t_inv = [t + _dot(t.astype(BF16), nb) for t, nb in zip(t_inv, n_blk)]
        else:
            prod = [_dot(jnp.concatenate([n, t], axis=0).astype(BF16), nb)
                    for n, t, nb in zip(n_pow, t_inv, n_blk)]
            n_pow = [pq[:c] for pq in prod]
            t_inv = [t + pq[c:] for t, pq in zip(t_inv, prod)]
    y = [_dot(t.astype(BF16),
              jnp.concatenate([_pair_blocks(mv, m0), _pair_blocks(p["kk_t"], m0)], axis=1))
         for t, mv, p in zip(t_inv, m_kv, pre)]
    u_loc = [yi[:, :LANES] for yi in y]
    q = [yi[:, LANES:] for yi in y]

    o_loc = [_dot(jnp.concatenate([ak, ab], axis=1).astype(BF16),
                  jnp.concatenate([p["v_blk"], _pair_blocks(-u, m0)], axis=0))
             for ak, ab, u, p in zip(a_rk, a_rb, u_loc, pre)]
    r_eff = [p["r_t"] - _dot(ab.astype(BF16), _pair_blocks(qi, m0)) for p, ab, qi in zip(pre, a_rb, q)]
    d_s = [jnp.where(blockdiag,
                     _dot_tn(jnp.concatenate([p["v"], u], axis=0).astype(BF16),
                             jnp.concatenate([p["k_p"], -p["b_p"]], axis=0).astype(BF16)), zero2)
           for p, u in zip(pre, u_loc)]
    qtb = [jnp.where(blockdiag, _dot_tn(qi.astype(BF16), p["b_p"].astype(BF16)), zero2)
           for p, qi in zip(pre, q)]
    return [(re, ol, qb, ds, p["p_last"]) for re, ol, qb, ds, p in zip(r_eff, o_loc, qtb, d_s, pre)]


def _wkv_kernel(xs_ref, s0_ref, wwa_ref, w0_ref, a0_ref, gup_ref,
                kk_ref, ka_ref, rk_ref, lng_ref, lnb_ref,
                y_ref, sout_ref,
                s_scr, r_scr, k_scr, v_scr, kks_scr, b_scr, lw_scr, o_scr,
                reff_scr, qtb_scr, ds_scr, pc_scr):
    rows = xs_ref.shape[1]
    rows_pad = r_scr.shape[0]
    n_chunks = rows_pad // CHUNK
    t_idx = pl.program_id(1)

    @pl.when(t_idx == 0)
    def _():
        s_scr[...] = s0_ref[0]

    xs = xs_ref[0].astype(F32)
    r = xs[:, 0:D_R]
    k = xs[:, D_R:2 * D_R]
    v = xs[:, 2 * D_R:3 * D_R]
    x_wa = xs[:, 3 * D_R:3 * D_R + LANES]
    x_g = xs[:, 3 * D_R + LANES:R_PROJ]

    lane = lax.broadcasted_iota(jnp.int32, (rows, LANES), 1)
    wa_in = jnp.where(lane < LORA_W, jnp.tanh(x_wa), x_wa).astype(BF16)
    wa = _dot(wa_in, wwa_ref[...])
    lw = -math.exp(-0.5) * _sigmoid(wa[:, 0:D_R] + w0_ref[...])
    a = _sigmoid(wa[:, D_R:2 * D_R] + a0_ref[...])
    g = _dot(_sigmoid(x_g).astype(BF16), gup_ref[...])

    hr = lax.broadcasted_iota(jnp.int32, (D_R, D_R), 0) // HEAD_DIM
    hc = lax.broadcasted_iota(jnp.int32, (D_R, D_R), 1) // HEAD_DIM
    head_ones = jnp.where(hr == hc, 1.0, 0.0).astype(BF16)

    kk = k * kk_ref[...]
    kk = kk * lax.rsqrt(jnp.maximum(_split_dot(kk * kk, head_ones, HEAD_SUM_TERMS), 1e-24))
    k2 = k * (1.0 + (a - 1.0) * ka_ref[...])
    b = a * kk

    def put(ref, val):
        ref[0:rows, :] = val
        if rows_pad > rows:
            ref[rows:rows_pad, :] = jnp.zeros((rows_pad - rows, D_R), F32)

    put(r_scr, r)
    put(k_scr, k2)
    put(v_scr, v)
    put(kks_scr, kk)
    put(b_scr, b)
    put(lw_scr, lw)

    ci = lax.broadcasted_iota(jnp.int32, (CHUNK, CHUNK), 0)
    cj = lax.broadcasted_iota(jnp.int32, (CHUNK, CHUNK), 1)
    tri = jnp.where(cj <= ci, 1.0, 0.0).astype(BF16)
    lane_c = lax.broadcasted_iota(jnp.int32, (CHUNK, LANES), 1)
    row_c = lax.broadcasted_iota(jnp.int32, (CHUNK, LANES), 0)
    m0 = lane_c < HEAD_DIM
    jj = jnp.where(m0, lane_c, lane_c - HEAD_DIM)
    br = lax.broadcasted_iota(jnp.int32, (LANES, LANES), 0) < HEAD_DIM
    bc = lax.broadcasted_iota(jnp.int32, (LANES, LANES), 1) < HEAD_DIM
    masks = (m0, jj < row_c, jj <= row_c, br == bc)

    group = math.gcd(n_chunks, WKV_GROUP_CHUNKS)

    def local_body(gi, carry):
        units = []
        for cc in range(group):
            r0 = pl.multiple_of((gi * group + cc) * CHUNK, CHUNK)
            sl = pl.ds(r0, CHUNK)
            lw_c = lw_scr[sl, :]
            cum = _split_dot_lhs(tri, lw_c)
            for pr in range(N_PAIRS):
                ls = slice(pr * LANES, (pr + 1) * LANES)
                units.append((r_scr[sl, ls], k_scr[sl, ls], v_scr[sl, ls], kks_scr[sl, ls],
                              b_scr[sl, ls], lw_c[:, ls], cum[:, ls]))
        res = _wkv_chunk_local(units, masks)
        for cc in range(group):
            ci_ = gi * group + cc
            sl = pl.ds(pl.multiple_of(ci_ * CHUNK, CHUNK), CHUNK)
            for pr in range(N_PAIRS):
                r_eff, o_loc, qtb, d_s, p_last = res[cc * N_PAIRS + pr]
                u = ci_ * N_PAIRS + pr
                reff_scr[u] = r_eff.astype(BF16)
                o_scr[sl, pr * LANES:(pr + 1) * LANES] = o_loc
                qtb_scr[u] = qtb.astype(BF16)
                ds_scr[u] = d_s
                pc_scr[u] = jnp.broadcast_to(p_last, (8, LANES))
        return carry

    lax.fori_loop(0, n_chunks // group, local_body, 0)

    def state_body(ci_, carry):
        sl = pl.ds(pl.multiple_of(ci_ * CHUNK, CHUNK), CHUNK)
        s_old = [s_scr[pr] for pr in range(N_PAIRS)]
        s_bf = [s.astype(BF16) for s in s_old]
        us = [ci_ * N_PAIRS + pr for pr in range(N_PAIRS)]
        o_add = [_dot_nt(reff_scr[u], sb) for u, sb in zip(us, s_bf)]
        s_mix = [_dot(sb, qtb_scr[u]) for u, sb in zip(us, s_bf)]
        for pr in range(N_PAIRS):
            ls = slice(pr * LANES, (pr + 1) * LANES)
            o_scr[sl, ls] = o_scr[sl, ls] + o_add[pr]
            s_scr[pr] = s_old[pr] * pc_scr[us[pr]][0:1, :] + ds_scr[us[pr]] - s_mix[pr]
        return carry

    lax.fori_loop(0, n_chunks, state_body, 0)

    o = o_scr[0:rows, :]
    inv_n = 1.0 / HEAD_DIM
    mean = _split_dot(o, head_ones, HEAD_SUM_TERMS) * inv_n
    cen = o - mean
    var = _split_dot(cen * cen, head_ones, HEAD_SUM_TERMS) * inv_n
    o_n = cen * lax.rsqrt(var + GN_EPS) * lng_ref[...] + lnb_ref[...]
    bonus = _split_dot(r * k2 * rk_ref[...], head_ones, HEAD_SUM_TERMS) * v
    y_ref[0] = ((o_n + bonus) * g).astype(BF16)

    @pl.when(t_idx == pl.num_programs(1) - 1)
    def _():
        sout_ref[0] = s_scr[...]


def _split_dot_lhs(tri, x):
    hi = x.astype(BF16)
    r1 = x - hi.astype(F32)
    mid = r1.astype(BF16)
    lo = (r1 - mid.astype(F32)).astype(BF16)
    return _dot(tri, hi) + _dot(tri, mid) + _dot(tri, lo)


def _wkv(xs, s0_pairs, wp, *, rows):
    b, t, _ = xs.shape
    rows_pad = -(-rows // CHUNK) * CHUNK
    n_units = rows_pad // CHUNK * N_PAIRS
    grid = (b, t // rows)
    vec = lambda n: _const_spec((1, n))
    scr = lambda: pltpu.VMEM((rows_pad, D_R), F32)
    return pl.pallas_call(
        _wkv_kernel,
        grid=grid,
        in_specs=[pl.BlockSpec((1, rows, R_PROJ), lambda i, j: (i, j, 0)),
                  pl.BlockSpec((1, N_PAIRS, LANES, LANES), lambda i, j: (i, 0, 0, 0)),
                  _const_spec((LANES, 2 * D_R)),
                  vec(D_R), vec(D_R),
                  _const_spec((LORA_G, D_R)),
                  vec(D_R), vec(D_R), vec(D_R), vec(D_R), vec(D_R)],
        out_specs=(pl.BlockSpec((1, rows, D_R), lambda i, j: (i, j, 0)),
                   pl.BlockSpec((1, N_PAIRS, LANES, LANES), lambda i, j: (i, 0, 0, 0))),
        out_shape=(jax.ShapeDtypeStruct((b, t, D_R), BF16),
                   jax.ShapeDtypeStruct((b, N_PAIRS, LANES, LANES), F32)),
        scratch_shapes=[pltpu.VMEM((N_PAIRS, LANES, LANES), F32),
                        scr(), scr(), scr(), scr(), scr(), scr(), scr(),
                        pltpu.VMEM((n_units, CHUNK, LANES), BF16),
                        pltpu.VMEM((n_units, LANES, LANES), BF16),
                        pltpu.VMEM((n_units, LANES, LANES), F32),
                        pltpu.VMEM((n_units, 8, LANES), F32)],
        compiler_params=pltpu.CompilerParams(
            dimension_semantics=("arbitrary", "arbitrary"), vmem_limit_bytes=VMEM_LIMIT),
        name="wkv",
    )(xs, s0_pairs, *wp)


def _state_to_pairs(s):
    b = s.shape[0]
    s = s.reshape(b, N_PAIRS, 2, HEAD_DIM, HEAD_DIM)
    z = jnp.zeros_like(s[:, :, 0])
    top = jnp.concatenate([s[:, :, 0], z], axis=-1)
    bot = jnp.concatenate([z, s[:, :, 1]], axis=-1)
    return jnp.concatenate([top, bot], axis=-2)


def _pairs_to_state(sp):
    b = sp.shape[0]
    h0 = sp[:, :, :HEAD_DIM, :HEAD_DIM]
    h1 = sp[:, :, HEAD_DIM:, HEAD_DIM:]
    return jnp.stack([h0, h1], axis=2).reshape(b, N_HEADS, HEAD_DIM, HEAD_DIM)


def _softmax_pv(units, m0):
    mx = []
    for parts in units:
        m = None
        for s, _ in parts:
            mi = jnp.max(s, axis=-1, keepdims=True)
            m = mi if m is None else jnp.maximum(m, mi)
        mx.append(m)
    es = [[jnp.exp2(s - m) for s, _ in parts] for parts, m in zip(units, mx)]
    ls = [functools.reduce(lambda a, b: a + b, [jnp.sum(e, axis=-1, keepdims=True) for e in ep])
          for ep in es]
    pvs = [functools.reduce(lambda a, b: a + b,
                            [_dot(e.astype(BF16), vals) for e, (_, vals) in zip(ep, parts)])
           for ep, parts in zip(es, units)]
    out = []
    for pv, l in zip(pvs, ls):
        pv = pv / l
        c = pv.shape[0] // 2
        out.append(jnp.where(m0, pv[:c], pv[c:]))
    return out


def _pair_queries(q, m0):
    z = jnp.zeros_like(q)
    return jnp.concatenate([jnp.where(m0, q, z), jnp.where(m0, z, q)], axis=0)


def _attn_prompt_kernel(q_ref, kc_ref, kp_ref, vc_ref, vp_ref, bias_ref, y_ref, k_scr, v_scr):
    rows = q_ref.shape[1]
    n_chunks = rows // CHUNK
    t_idx = pl.program_id(1)
    k_scr[0:WINDOW, :] = kp_ref[0]
    k_scr[WINDOW:WINDOW + rows, :] = kc_ref[0]
    v_scr[0:WINDOW, :] = vp_ref[0]
    v_scr[WINDOW:WINDOW + rows, :] = vc_ref[0]
    m0 = lax.broadcasted_iota(jnp.int32, (CHUNK, LANES), 1) < HEAD_DIM
    key_i = lax.broadcasted_iota(jnp.int32, (1, BAND), 1)
    ri = lax.broadcasted_iota(jnp.int32, (2 * CHUNK, 2 * CHUNK), 0)
    rj = lax.broadcasted_iota(jnp.int32, (2 * CHUNK, 2 * CHUNK), 1)
    eye = jnp.where(ri == rj, 1.0, 0.0).astype(BF16)

    def chunk_body(c, carry, *, masked):
        r0 = pl.multiple_of(c * CHUNK, CHUNK)
        band = pl.ds(r0, BAND)
        units = []
        for pr in range(N_PAIRS):
            ls = slice(pr * LANES, (pr + 1) * LANES)
            lhs = jnp.concatenate([_pair_queries(q_ref[0, pl.ds(r0, CHUNK), ls], m0), eye], axis=1)
            rhs = jnp.concatenate([k_scr[band, ls], bias_ref[pr]], axis=1)
            s = _dot_nt(lhs, rhs)
            if masked:
                s = s + jnp.where(key_i + (r0 - WINDOW) >= 0, 0.0, NEG_BIG)
            units.append([(s, v_scr[band, ls])])
        for pr, o in enumerate(_softmax_pv(units, m0)):
            y_ref[0, pl.ds(r0, CHUNK), pr * LANES:(pr + 1) * LANES] = o.astype(BF16)
        return carry

    @pl.when(t_idx == 0)
    def _():
        lax.fori_loop(0, n_chunks, functools.partial(chunk_body, masked=True), 0)

    @pl.when(t_idx > 0)
    def _():
        lax.fori_loop(0, n_chunks, functools.partial(chunk_body, masked=False), 0)


def _attn_prompt(q, k, v, bias_pairs, *, rows):
    b, t, _ = q.shape
    assert rows == WINDOW
    cur = pl.BlockSpec((1, rows, D_A), lambda i, j: (i, j, 0))
    prev = pl.BlockSpec((1, rows, D_A), lambda i, j: (i, jnp.maximum(j - 1, 0), 0))
    return pl.pallas_call(
        _attn_prompt_kernel,
        grid=(b, t // rows),
        in_specs=[cur, cur, prev, cur, prev, _const_spec((N_PAIRS, BAND, LANES))],
        out_specs=cur,
        out_shape=jax.ShapeDtypeStruct((b, t, D_A), BF16),
        scratch_shapes=[pltpu.VMEM((WINDOW + rows, D_A), BF16),
                        pltpu.VMEM((WINDOW + rows, D_A), BF16)],
        compiler_params=pltpu.CompilerParams(
            dimension_semantics=("arbitrary", "arbitrary"), vmem_limit_bytes=VMEM_LIMIT),
        name="attn_prompt",
    )(q, k, k, v, v, bias_pairs)


def _attn_sample_kernel(q_ref, k_ref, v_ref, ck_ref, cv_ref, bias_c_ref, bias_n_ref, y_ref):
    rows = q_ref.shape[1]
    m0 = lax.broadcasted_iota(jnp.int32, (rows, LANES), 1) < HEAD_DIM
    units = []
    for pr in range(N_PAIRS):
        ls = slice(pr * LANES, (pr + 1) * LANES)
        lhs = _pair_queries(q_ref[0, :, ls], m0)
        s_c = _dot_nt(lhs, ck_ref[0, :, ls].astype(BF16)) + bias_c_ref[pr]
        s_n = _dot_nt(lhs, k_ref[0, :, ls]) + bias_n_ref[pr]
        units.append([(s_c, cv_ref[0, :, ls].astype(BF16)), (s_n, v_ref[0, :, ls])])
    for pr, o in enumerate(_softmax_pv(units, m0)):
        y_ref[0, :, pr * LANES:(pr + 1) * LANES] = o.astype(BF16)


def _attn_sample(q, k, v, cache_k, cache_v, bias_c, bias_n):
    b, rows, _ = q.shape
    n_cache = cache_k.shape[1]
    cur = pl.BlockSpec((1, rows, D_A), lambda i: (i, 0, 0))
    cache = pl.BlockSpec((1, n_cache, D_A), lambda i: (i, 0, 0))
    return pl.pallas_call(
        _attn_sample_kernel,
        grid=(b,),
        in_specs=[cur, cur, cur, cache, cache,
                  _const_spec((N_PAIRS, 2 * rows, n_cache)), _const_spec((N_PAIRS, 2 * rows, rows))],
        out_specs=cur,
        out_shape=jax.ShapeDtypeStruct((b, rows, D_A), BF16),
        compiler_params=pltpu.CompilerParams(dimension_semantics=("arbitrary",)),
        name="attn_sample",
    )(q, k, v, cache_k, cache_v, bias_c, bias_n)


def _rel_bias(table, n_q, n_k, offset):
    h = table.shape[0]
    period = n_q + n_k
    e = jnp.arange(period)
    e = jnp.where(e < n_k, e, e - period)
    u = table[:, jnp.clip(offset - e, -REL_CLIP, REL_CLIP) + REL_CLIP].astype(F32) * LOG2E
    toeplitz = jnp.tile(u, (1, n_q))[:, :n_q * (period - 1)].reshape(h, n_q, period - 1)
    return toeplitz[:, :, :n_k]


def _pair_rows(bias):
    h, nq, nk = bias.shape
    return bias.reshape(h // 2, 2 * nq, nk)


def _out_kernel(x_ref, yr_ref, ya_ref, gt_ref, mod_ref, wbr_ref, wba_ref, wo_ref, ng_ref,
                w1_ref, w2_ref, fg_ref, y_ref):
    nb, rows, d = x_ref.shape
    n = nb * rows
    mod = mod_ref[...]
    g1 = mod[:, :, 2 * d:3 * d]
    sh2 = mod[:, :, 3 * d:4 * d]
    sc2 = mod[:, :, 4 * d:5 * d]
    g2 = mod[:, :, 5 * d:6 * d]

    gates = gt_ref[...].reshape(n, 2 * d).astype(F32)
    y_r = yr_ref[...].reshape(n, D_R)
    y_a = ya_ref[...].reshape(n, D_A)
    merged = (jax.nn.sigmoid(gates[:, 0:d]) * _dot(y_r, wbr_ref[...])
              + jax.nn.sigmoid(gates[:, d:2 * d]) * _dot(y_a, wba_ref[...]))
    mix = _dot(merged.astype(BF16), wo_ref[...]).reshape(nb, rows, d)
    x1 = x_ref[...] + g1 * mix

    h2 = ((_rms(x1) * ng_ref[...]) * (1.0 + sc2) + sh2).reshape(n, d).astype(BF16)
    ff_blk = 1024
    acc = None
    for j in range(D_FF // ff_blk):
        mid = _dot(h2, w1_ref[:, j * ff_blk:(j + 1) * ff_blk])
        act = jnp.square(jnp.maximum(mid, 0.0)).astype(BF16)
        part = _dot(act, w2_ref[j * ff_blk:(j + 1) * ff_blk, :])
        acc = part if acc is None else acc + part
    x2 = x1 + g2 * acc.reshape(nb, rows, d)
    y_ref[...] = _rms(x2) * fg_ref[...]


def _out(x, y_r, y_a, gates, mod, w_br_r, w_br_a, w_out, norm_g, w_ff1, w_ff2, final_g, *, nb, rows):
    b, t, d = x.shape
    tile = lambda n: pl.BlockSpec((nb, rows, n), lambda i, j: (i, j, 0))
    return pl.pallas_call(
        _out_kernel,
        grid=(b // nb, t // rows),
        in_specs=[tile(d), tile(D_R), tile(D_A), tile(2 * d),
                  pl.BlockSpec((nb, 1, 6 * d), lambda i, j: (i, 0, 0)),
                  _const_spec((D_R, d)), _const_spec((D_A, d)), _const_spec((d, d)),
                  _const_spec((1, 1, d)),
                  _const_spec((d, D_FF)), _const_spec((D_FF, d)),
                  _const_spec((1, 1, d))],
        out_specs=tile(d),
        out_shape=jax.ShapeDtypeStruct((b, t, d), F32),
        compiler_params=pltpu.CompilerParams(
            dimension_semantics=("arbitrary", "arbitrary"), vmem_limit_bytes=VMEM_LIMIT),
        name="out",
    )(x, y_r, y_a, gates, mod, w_br_r, w_br_a, w_out, norm_g.reshape(1, 1, d),
      w_ff1, w_ff2, final_g.reshape(1, 1, d))


def _group(x, mod, prev0, s0, attend, W, *, nb, rows, n_tail):
    b, t, d = x.shape
    xs, q, k, v, gates, p_tail, k_tail, v_tail = _proj(
        x, mod, W["norm_mix_g"], W["w_in"], prev0, W["mu_shift"], nb=nb, rows=rows, n_tail=n_tail)
    y_r, s_pairs = _wkv(xs, _state_to_pairs(s0), W["wkv"], rows=rows)
    y_a = attend(q, k, v)
    y = _out(x, y_r, y_a, gates, mod, W["w_br_r"], W["w_br_a"], W["w_out"], W["norm_mlp_g"],
             W["w_ff1"], W["w_ff2"], W["final_norm_g"], nb=nb, rows=rows)
    return (y, _pairs_to_state(s_pairs)[None], p_tail[None, :, 7, :],
            k_tail.reshape(1, b, n_tail, N_HEADS, HEAD_DIM),
            v_tail.reshape(1, b, n_tail, N_HEADS, HEAD_DIM))


def kernel(x_prompt, x_sample, c_prompt, c_sample, state_rwkv_wkv, state_rwkv_shift, cache_att_k, cache_att_v, w_ada, b_ada, norm_mix_g, w_in, mu_shift, w_lora_up, w0, a_lora_up, a0, g_lora_up, k_k, k_a, r_k, lnx_g, lnx_b, rel_table, w_br_r, w_br_a, w_out, norm_mlp_g, w_ff1, w_ff2, final_norm_g):
    bp, tp, d = x_prompt.shape
    bs, ts, _ = x_sample.shape
    l = 0
    row = lambda a: a.reshape(1, -1)
    zl = jnp.zeros((LORA_W, D_R), F32)
    w_wa = jnp.concatenate([jnp.concatenate([w_lora_up[l], zl], axis=1),
                            jnp.concatenate([zl, a_lora_up[l]], axis=1)], axis=0).astype(BF16)
    W = dict(
        norm_mix_g=norm_mix_g[l], w_in=w_in[l].astype(BF16), mu_shift=mu_shift[l],
        wkv=(w_wa, row(w0[l]), row(a0[l]), g_lora_up[l].astype(BF16),
             row(k_k[l]), row(k_a[l]), row(r_k[l]), row(lnx_g[l]), row(lnx_b[l])),
        w_br_r=w_br_r[l].astype(BF16), w_br_a=w_br_a[l].astype(BF16), w_out=w_out[l].astype(BF16),
        norm_mlp_g=norm_mlp_g[l], w_ff1=w_ff1[l].astype(BF16), w_ff2=w_ff2[l].astype(BF16),
        final_norm_g=final_norm_g)

    mod = _ada(jnp.concatenate([c_prompt, c_sample], axis=0), w_ada[l], b_ada[l])
    mod_p = mod[:bp, None, :]
    mod_s = mod[bp:, None, :]

    bias_p = _rel_bias(rel_table[l], CHUNK, BAND, WINDOW)
    bias_t = bias_p.reshape(N_PAIRS, 2 * CHUNK, BAND).transpose(0, 2, 1).astype(BF16)
    attend_p = lambda q, k, v: _attn_prompt(q, k, v, bias_t, rows=WINDOW)
    n_keep = min(WINDOW, tp)
    y_p, p_wkv, p_shift, p_k, p_v = _group(
        x_prompt, mod_p, jnp.zeros((bp, 8, R_PROJ), F32),
        jnp.zeros((bp, N_HEADS, HEAD_DIM, HEAD_DIM), F32), attend_p, W,
        nb=1, rows=WINDOW, n_tail=n_keep)

    n_cache = cache_att_k.shape[2]
    bias_s = _pair_rows(_rel_bias(rel_table[l], ts, n_cache + ts, n_cache))
    ck = cache_att_k[l].reshape(bs, n_cache, D_A)
    cv = cache_att_v[l].reshape(bs, n_cache, D_A)
    attend_s = lambda q, k, v: _attn_sample(q, k, v, ck, cv, bias_s[:, :, :n_cache], bias_s[:, :, n_cache:])
    prev_s = jnp.broadcast_to(state_rwkv_shift[l][:, None, :], (bs, 8, R_PROJ))
    y_s, s_wkv, s_shift, s_k, s_v = _group(
        x_sample, mod_s, prev_s, state_rwkv_wkv[l], attend_s, W, nb=bs, rows=ts, n_tail=ts)

    return (y_p, y_s, p_wkv, p_shift, p_k, p_v, s_wkv, s_shift, s_k, s_v)
```

```python
import functools
import math

import jax
import jax.numpy as jnp
from jax import lax
from jax.experimental import pallas as pl
from jax.experimental.pallas import tpu as pltpu

D_MODEL = 1024
HEAD_DIM = 64
N_HEADS = 8
D_R = N_HEADS * HEAD_DIM
D_A = N_HEADS * HEAD_DIM
LORA_W = 64
LORA_A = 64
LORA_G = 128
R_PROJ = 3 * D_R + LORA_W + LORA_A + LORA_G
CHUNK = 64
BAND_CHUNKS = 8
WINDOW = BAND_CHUNKS * CHUNK
BAND = WINDOW + CHUNK
REL_CLIP = 128
D_FF = 4 * D_MODEL
IN_PROJ = R_PROJ + 3 * D_A + 2 * D_MODEL
NORM_EPS = 1e-6
GN_EPS = 64e-5
ATT_SCALE = HEAD_DIM ** -0.5
LOG2E = math.log2(math.e)
PAST_LEN = 1024

LANES = 128
N_PAIRS = N_HEADS // 2
NEG_BIG = -1e30
HEAD_SUM_TERMS = 1
WKV_GROUP_CHUNKS = 4
ATTN_GROUP_CHUNKS = 4
VMEM_LIMIT = 56 * 1024 * 1024

F32 = jnp.float32
BF16 = jnp.bfloat16


def _dot(a, b):
    return lax.dot_general(a, b, (((1,), (0,)), ((), ())), preferred_element_type=F32)


def _dot_nt(a, b):
    return lax.dot_general(a, b, (((1,), (1,)), ((), ())), preferred_element_type=F32)


def _dot_tn(a, b):
    return lax.dot_general(a, b, (((0,), (0,)), ((), ())), preferred_element_type=F32)


def _split_dot(x, w, terms):
    acc = None
    rem = x
    for _ in range(terms):
        piece = rem.astype(BF16)
        part = _dot(piece, w)
        acc = part if acc is None else acc + part
        rem = rem - piece.astype(F32)
    return acc


def _sigmoid(x):
    return 0.5 * jnp.tanh(0.5 * x) + 0.5


def _rms(x):
    return x * lax.rsqrt(jnp.mean(x * x, axis=-1, keepdims=True) + NORM_EPS)


def _const_spec(shape):
    nd = len(shape)
    return pl.BlockSpec(shape, lambda *_: (0,) * nd, pipeline_mode=pl.Buffered(1))


def _ada_kernel(c_ref, w_ref, b_ref, o_ref):
    c = c_ref[...]
    s = (c * jax.nn.sigmoid(c)).astype(BF16)
    o_ref[...] = _dot(s, w_ref[...].astype(BF16)) + b_ref[...]


def _ada(c, w_ada, b_ada):
    n, d = c.shape
    n_out = w_ada.shape[1]
    bn = 1024
    return pl.pallas_call(
        _ada_kernel,
        grid=(n_out // bn,),
        in_specs=[pl.BlockSpec((n, d), lambda j: (0, 0)),
                  pl.BlockSpec((d, bn), lambda j: (0, j)),
                  pl.BlockSpec((1, bn), lambda j: (0, j))],
        out_specs=pl.BlockSpec((n, bn), lambda j: (0, j)),
        out_shape=jax.ShapeDtypeStruct((n, n_out), F32),
        compiler_params=pltpu.CompilerParams(dimension_semantics=("arbitrary",)),
        name="ada",
    )(c, w_ada, b_ada.reshape(1, n_out))


def _proj_kernel(x_ref, mod_ref, g_ref, w_ref, prev0_ref, mu_ref,
                 xs_ref, q_ref, k_ref, v_ref, gt_ref, ptail_ref, ktail_ref, vtail_ref,
                 h_scr, prev_scr, *, n_tail, tiles_per_seq):
    nb, rows, d = x_ref.shape
    step = pl.program_id(0)
    t_out = jnp.maximum(step - 1, 0) % tiles_per_seq

    @pl.when(step == 0)
    def _():
        h_scr[...] = jnp.zeros(h_scr.shape, BF16)

    @pl.when(t_out == 0)
    def _():
        prev_scr[...] = prev0_ref[...]

    mod = mod_ref[...]
    sh = mod[:, :, 0:d]
    sc = mod[:, :, d:2 * d]
    h_next = ((_rms(x_ref[...]) * g_ref[...]) * (1.0 + sc) + sh).reshape(nb * rows, d).astype(BF16)

    def col(lo, hi):
        return _dot(h_scr[...], w_ref[:, lo:hi])

    first_row = lax.broadcasted_iota(jnp.int32, (1, rows, 1), 1) == 0

    def shift_mix(lo, hi):
        def finish(p2):
            cs = slice(lo, hi)
            p = p2.reshape(nb, rows, hi - lo)
            shifted = pltpu.roll(p2, 1, axis=0).reshape(nb, rows, hi - lo)
            p_prev = jnp.where(first_row, prev_scr[:, 0:1, cs], shifted)
            prev_scr[:, 0:1, cs] = p[:, rows - 1:rows, :]
            xs_ref[:, :, cs] = (p + (p_prev - p) * mu_ref[:, :, cs]).astype(BF16)
            ptail_ref[:, :, cs] = p[:, rows - 8:, :]
        return finish

    def plain(ref, cs, scale=None, tail=None):
        def finish(t2):
            t = t2.reshape(nb, rows, t2.shape[1])
            ref[:, :, cs] = (t if scale is None else t * scale).astype(BF16)
            if tail is not None:
                tail[...] = t[:, rows - n_tail:, :]
        return finish

    half = R_PROJ // 2
    g0 = R_PROJ + 3 * D_A
    full = slice(None)
    jobs = [
        (0, half, shift_mix(0, half)),
        (half, R_PROJ, shift_mix(half, R_PROJ)),
        (R_PROJ, R_PROJ + D_A, plain(q_ref, full, scale=ATT_SCALE * LOG2E)),
        (R_PROJ + D_A, R_PROJ + 2 * D_A, plain(k_ref, full, tail=ktail_ref)),
        (R_PROJ + 2 * D_A, g0, plain(v_ref, full, tail=vtail_ref)),
        (g0, g0 + d, plain(gt_ref, slice(0, d))),
        (g0 + d, g0 + 2 * d, plain(gt_ref, slice(d, 2 * d))),
    ]
    pending = None
    for lo, hi, finish in jobs:
        res = col(lo, hi)
        if pending is not None:
            pending[0](pending[1])
        pending = (finish, res)
    pending[0](pending[1])

    h_scr[...] = h_next


def _proj(x, mod, norm_g, w_in, prev0, mu, *, nb, rows, n_tail):
    b, t, d = x.shape
    tiles_per_seq = t // rows
    n_tiles = (b // nb) * tiles_per_seq
    t_in = lambda s: jnp.minimum(s, n_tiles - 1)
    t_out = lambda s: jnp.maximum(s - 1, 0)
    tile_in = lambda n: pl.BlockSpec(
        (nb, rows, n), lambda s: (t_in(s) // tiles_per_seq, t_in(s) % tiles_per_seq, 0))
    tile = lambda n: pl.BlockSpec(
        (nb, rows, n), lambda s: (t_out(s) // tiles_per_seq, t_out(s) % tiles_per_seq, 0))
    tail = lambda r, n: pl.BlockSpec((nb, r, n), lambda s: (t_out(s) // tiles_per_seq, 0, 0))
    out_shape = (
        jax.ShapeDtypeStruct((b, t, R_PROJ), BF16),
        jax.ShapeDtypeStruct((b, t, D_A), BF16),
        jax.ShapeDtypeStruct((b, t, D_A), BF16),
        jax.ShapeDtypeStruct((b, t, D_A), BF16),
        jax.ShapeDtypeStruct((b, t, 2 * d), BF16),
        jax.ShapeDtypeStruct((b, 8, R_PROJ), F32),
        jax.ShapeDtypeStruct((b, n_tail, D_A), F32),
        jax.ShapeDtypeStruct((b, n_tail, D_A), F32),
    )
    return pl.pallas_call(
        functools.partial(_proj_kernel, n_tail=n_tail, tiles_per_seq=tiles_per_seq),
        grid=(n_tiles + 1,),
        in_specs=[tile_in(d),
                  pl.BlockSpec((nb, 1, 6 * d), lambda s: (t_in(s) // tiles_per_seq, 0, 0)),
                  _const_spec((1, 1, d)),
                  _const_spec((d, IN_PROJ)),
                  tail(8, R_PROJ),
                  _const_spec((1, 1, R_PROJ))],
        out_specs=(tile(R_PROJ), tile(D_A), tile(D_A), tile(D_A), tile(2 * d),
                   tail(8, R_PROJ), tail(n_tail, D_A), tail(n_tail, D_A)),
        out_shape=out_shape,
        scratch_shapes=[pltpu.VMEM((nb * rows, d), BF16),
                        pltpu.VMEM((nb, 8, R_PROJ), F32)],
        compiler_params=pltpu.CompilerParams(
            dimension_semantics=("arbitrary",), vmem_limit_bytes=VMEM_LIMIT),
        name="proj",
    )(x, mod, norm_g.reshape(1, 1, d), w_in, prev0, mu.reshape(1, 1, R_PROJ))


def _pair_blocks(x, m0):
    xb = x.astype(BF16)
    z = jnp.zeros_like(xb)
    return jnp.concatenate([jnp.where(m0, xb, z), jnp.where(m0, z, xb)], axis=0)


def _wkv_chunk_local(units, masks):
    m0, strict, incl, blockdiag = masks
    c = CHUNK
    eye = jnp.where(incl, 1.0, 0.0) - jnp.where(strict, 1.0, 0.0)
    zero = jnp.zeros((c, LANES), F32)
    zero2 = jnp.zeros((LANES, LANES), F32)
    pre = []
    for r, k, v, kk, b, lw, cum in units:
        r_t = r * jnp.exp(cum)
        kk_t = kk * jnp.exp(cum - lw)
        e_inv = jnp.exp(-cum)
        cum_last = cum[c - 1:c, :]
        dec = jnp.exp(cum_last - cum)
        pre.append(dict(
            r_t=r_t, kk_t=kk_t, v=v, lhs=jnp.concatenate([kk_t, r_t], axis=0).astype(BF16),
            k_h=_pair_blocks(k * e_inv, m0), b_h=_pair_blocks(b * e_inv, m0),
            v_blk=_pair_blocks(v, m0), k_p=k * dec, b_p=b * dec, p_last=jnp.exp(cum_last)))
    g1 = [_dot_nt(p["lhs"], p["k_h"]) for p in pre]
    g2 = [_dot_nt(p["lhs"], p["b_h"]) for p in pre]
    m_k = [jnp.where(strict, g[:c], zero).astype(BF16) for g in g1]
    a_rk = [jnp.where(incl, g[c:], zero) for g in g1]
    n_pow = [jnp.where(strict, g[:c], zero) for g in g2]
    a_rb = [jnp.where(incl, g[c:], zero) for g in g2]
    m_kv = [_dot(m, p["v_blk"]) for m, p in zip(m_k, pre)]

    t_inv = [eye - n for n in n_pow]
    n_pow = [_dot(n.astype(BF16), _pair_blocks(n, m0)) for n in n_pow]
    n_levels = int(math.log2(c)) - 1
    for level in range(n_levels):
        n_blk = [_pair_blocks(n, m0) for n in n_pow]
        if level == n_levels - 1:
            t_inv = [t + _dot(t.astype(BF16), nb) for t, nb in zip(t_inv, n_blk)]
        else:
            prod = [_dot(jnp.concatenate([n, t], axis=0).astype(BF16), nb)
                    for n, t, nb in zip(n_pow, t_inv, n_blk)]
            n_pow = [pq[:c] for pq in prod]
            t_inv = [t + pq[c:] for t, pq in zip(t_inv, prod)]
    y = [_dot(t.astype(BF16),
              jnp.concatenate([_pair_blocks(mv, m0), _pair_blocks(p["kk_t"], m0)], axis=1))
         for t, mv, p in zip(t_inv, m_kv, pre)]
    u_loc = [yi[:, :LANES] for yi in y]
    q = [yi[:, LANES:] for yi in y]

    o_loc = [_dot(jnp.concatenate([ak, ab], axis=1).astype(BF16),
                  jnp.concatenate([p["v_blk"], _pair_blocks(-u, m0)], axis=0))
             for ak, ab, u, p in zip(a_rk, a_rb, u_loc, pre)]
    r_eff = [p["r_t"] - _dot(ab.astype(BF16), _pair_blocks(qi, m0)) for p, ab, qi in zip(pre, a_rb, q)]
    d_s = [jnp.where(blockdiag,
                     _dot_tn(jnp.concatenate([p["v"], u], axis=0).astype(BF16),
                             jnp.concatenate([p["k_p"], -p["b_p"]], axis=0).astype(BF16)), zero2)
           for p, u in zip(pre, u_loc)]
    qtb = [jnp.where(blockdiag, _dot_tn(qi.astype(BF16), p["b_p"].astype(BF16)), zero2)
           for p, qi in zip(pre, q)]
    return [(re, ol, qb, ds, p["p_last"]) for re, ol, qb, ds, p in zip(r_eff, o_loc, qtb, d_s, pre)]


def _wkv_kernel(xs_ref, s0_ref, wwa_ref, w0_ref, a0_ref, gup_ref,
                kk_ref, ka_ref, rk_ref, lng_ref, lnb_ref,
                y_ref, sout_ref,
                s_scr, r_scr, k_scr, v_scr, kks_scr, b_scr, lw_scr, o_scr,
                reff_scr, qtb_scr, ds_scr, pc_scr):
    rows = xs_ref.shape[1]
    rows_pad = r_scr.shape[0]
    n_chunks = rows_pad // CHUNK
    t_idx = pl.program_id(1)

    @pl.when(t_idx == 0)
    def _():
        s_scr[...] = s0_ref[0]

    xs = xs_ref[0].astype(F32)
    r = xs[:, 0:D_R]
    k = xs[:, D_R:2 * D_R]
    v = xs[:, 2 * D_R:3 * D_R]
    x_wa = xs[:, 3 * D_R:3 * D_R + LANES]
    x_g = xs[:, 3 * D_R + LANES:R_PROJ]

    lane = lax.broadcasted_iota(jnp.int32, (rows, LANES), 1)
    wa_in = jnp.where(lane < LORA_W, jnp.tanh(x_wa), x_wa).astype(BF16)
    wa = _dot(wa_in, wwa_ref[...])
    lw = -math.exp(-0.5) * _sigmoid(wa[:, 0:D_R] + w0_ref[...])
    a = _sigmoid(wa[:, D_R:2 * D_R] + a0_ref[...])
    g = _dot(_sigmoid(x_g).astype(BF16), gup_ref[...])

    hr = lax.broadcasted_iota(jnp.int32, (D_R, D_R), 0) // HEAD_DIM
    hc = lax.broadcasted_iota(jnp.int32, (D_R, D_R), 1) // HEAD_DIM
    head_ones = jnp.where(hr == hc, 1.0, 0.0).astype(BF16)

    kk = k * kk_ref[...]
    kk = kk * lax.rsqrt(jnp.maximum(_split_dot(kk * kk, head_ones, HEAD_SUM_TERMS), 1e-24))
    k2 = k * (1.0 + (a - 1.0) * ka_ref[...])
    b = a * kk

    def put(ref, val):
        ref[0:rows, :] = val
        if rows_pad > rows:
            ref[rows:rows_pad, :] = jnp.zeros((rows_pad - rows, D_R), F32)

    put(r_scr, r)
    put(k_scr, k2)
    put(v_scr, v)
    put(kks_scr, kk)
    put(b_scr, b)
    put(lw_scr, lw)

    ci = lax.broadcasted_iota(jnp.int32, (CHUNK, CHUNK), 0)
    cj = lax.broadcasted_iota(jnp.int32, (CHUNK, CHUNK), 1)
    tri = jnp.where(cj <= ci, 1.0, 0.0).astype(BF16)
    lane_c = lax.broadcasted_iota(jnp.int32, (CHUNK, LANES), 1)
    row_c = lax.broadcasted_iota(jnp.int32, (CHUNK, LANES), 0)
    m0 = lane_c < HEAD_DIM
    jj = jnp.where(m0, lane_c, lane_c - HEAD_DIM)
    br = lax.broadcasted_iota(jnp.int32, (LANES, LANES), 0) < HEAD_DIM
    bc = lax.broadcasted_iota(jnp.int32, (LANES, LANES), 1) < HEAD_DIM
    masks = (m0, jj < row_c, jj <= row_c, br == bc)

    group = math.gcd(n_chunks, WKV_GROUP_CHUNKS)

    def local_body(gi, carry):
        units = []
        for cc in range(group):
            r0 = pl.multiple_of((gi * group + cc) * CHUNK, CHUNK)
            sl = pl.ds(r0, CHUNK)
            lw_c = lw_scr[sl, :]
            cum = _split_dot_lhs(tri, lw_c)
            for pr in range(N_PAIRS):
                ls = slice(pr * LANES, (pr + 1) * LANES)
                units.append((r_scr[sl, ls], k_scr[sl, ls], v_scr[sl, ls], kks_scr[sl, ls],
                              b_scr[sl, ls], lw_c[:, ls], cum[:, ls]))
        res = _wkv_chunk_local(units, masks)
        for cc in range(group):
            ci_ = gi * group + cc
            sl = pl.ds(pl.multiple_of(ci_ * CHUNK, CHUNK), CHUNK)
            for pr in range(N_PAIRS):
                r_eff, o_loc, qtb, d_s, p_last = res[cc * N_PAIRS + pr]
                u = ci_ * N_PAIRS + pr
                reff_scr[u] = r_eff.astype(BF16)
                o_scr[sl, pr * LANES:(pr + 1) * LANES] = o_loc
                qtb_scr[u] = qtb.astype(BF16)
                ds_scr[u] = d_s
                pc_scr[u] = jnp.broadcast_to(p_last, (8, LANES))
        return carry

    lax.fori_loop(0, n_chunks // group, local_body, 0)

    def state_body(ci_, carry):
        sl = pl.ds(pl.multiple_of(ci_ * CHUNK, CHUNK), CHUNK)
        s_old = [s_scr[pr] for pr in range(N_PAIRS)]
        s_bf = [s.astype(BF16) for s in s_old]
        us = [ci_ * N_PAIRS + pr for pr in range(N_PAIRS)]
        o_add = [_dot_nt(reff_scr[u], sb) for u, sb in zip(us, s_bf)]
        s_mix = [_dot(sb, qtb_scr[u]) for u, sb in zip(us, s_bf)]
        for pr in range(N_PAIRS):
            ls = slice(pr * LANES, (pr + 1) * LANES)
            o_scr[sl, ls] = o_scr[sl, ls] + o_add[pr]
            s_scr[pr] = s_old[pr] * pc_scr[us[pr]][0:1, :] + ds_scr[us[pr]] - s_mix[pr]
        return carry

    lax.fori_loop(0, n_chunks, state_body, 0)

    o = o_scr[0:rows, :]
    inv_n = 1.0 / HEAD_DIM
    mean = _split_dot(o, head_ones, HEAD_SUM_TERMS) * inv_n
    cen = o - mean
    var = _split_dot(cen * cen, head_ones, HEAD_SUM_TERMS) * inv_n
    o_n = cen * lax.rsqrt(var + GN_EPS) * lng_ref[...] + lnb_ref[...]
    bonus = _split_dot(r * k2 * rk_ref[...], head_ones, HEAD_SUM_TERMS) * v
    y_ref[0] = ((o_n + bonus) * g).astype(BF16)

    @pl.when(t_idx == pl.num_programs(1) - 1)
    def _():
        sout_ref[0] = s_scr[...]


def _split_dot_lhs(tri, x):
    hi = x.astype(BF16)
    r1 = x - hi.astype(F32)
    mid = r1.astype(BF16)
    lo = (r1 - mid.astype(F32)).astype(BF16)
    return _dot(tri, hi) + _dot(tri, mid) + _dot(tri, lo)


def _wkv(xs, s0_pairs, wp, *, rows):
    b, t, _ = xs.shape
    rows_pad = -(-rows // CHUNK) * CHUNK
    n_units = rows_pad // CHUNK * N_PAIRS
    grid = (b, t // rows)
    vec = lambda n: _const_spec((1, n))
    scr = lambda: pltpu.VMEM((rows_pad, D_R), F32)
    return pl.pallas_call(
        _wkv_kernel,
        grid=grid,
        in_specs=[pl.BlockSpec((1, rows, R_PROJ), lambda i, j: (i, j, 0)),
                  pl.BlockSpec((1, N_PAIRS, LANES, LANES), lambda i, j: (i, 0, 0, 0)),
                  _const_spec((LANES, 2 * D_R)),
                  vec(D_R), vec(D_R),
                  _const_spec((LORA_G, D_R)),
                  vec(D_R), vec(D_R), vec(D_R), vec(D_R), vec(D_R)],
        out_specs=(pl.BlockSpec((1, rows, D_R), lambda i, j: (i, j, 0)),
                   pl.BlockSpec((1, N_PAIRS, LANES, LANES), lambda i, j: (i, 0, 0, 0))),
        out_shape=(jax.ShapeDtypeStruct((b, t, D_R), BF16),
                   jax.ShapeDtypeStruct((b, N_PAIRS, LANES, LANES), F32)),
        scratch_shapes=[pltpu.VMEM((N_PAIRS, LANES, LANES), F32),
                        scr(), scr(), scr(), scr(), scr(), scr(), scr(),
                        pltpu.VMEM((n_units, CHUNK, LANES), BF16),
                        pltpu.VMEM((n_units, LANES, LANES), BF16),
                        pltpu.VMEM((n_units, LANES, LANES), F32),
                        pltpu.VMEM((n_units, 8, LANES), F32)],
        compiler_params=pltpu.CompilerParams(
            dimension_semantics=("arbitrary", "arbitrary"), vmem_limit_bytes=VMEM_LIMIT),
        name="wkv",
    )(xs, s0_pairs, *wp)


def _state_to_pairs(s):
    b = s.shape[0]
    s = s.reshape(b, N_PAIRS, 2, HEAD_DIM, HEAD_DIM)
    z = jnp.zeros_like(s[:, :, 0])
    top = jnp.concatenate([s[:, :, 0], z], axis=-1)
    bot = jnp.concatenate([z, s[:, :, 1]], axis=-1)
    return jnp.concatenate([top, bot], axis=-2)


def _pairs_to_state(sp):
    b = sp.shape[0]
    h0 = sp[:, :, :HEAD_DIM, :HEAD_DIM]
    h1 = sp[:, :, HEAD_DIM:, HEAD_DIM:]
    return jnp.stack([h0, h1], axis=2).reshape(b, N_HEADS, HEAD_DIM, HEAD_DIM)


def _softmax_pv(units, m0):
    mx = []
    for parts in units:
        m = None
        for s, _ in parts:
            mi = jnp.max(s, axis=-1, keepdims=True)
            m = mi if m is None else jnp.maximum(m, mi)
        mx.append(m)
    es = [[jnp.exp2(s - m) for s, _ in parts] for parts, m in zip(units, mx)]
    ls = [functools.reduce(lambda a, b: a + b, [jnp.sum(e, axis=-1, keepdims=True) for e in ep])
          for ep in es]
    pvs = [functools.reduce(lambda a, b: a + b,
                            [_dot(e.astype(BF16), vals) for e, (_, vals) in zip(ep, parts)])
           for ep, parts in zip(es, units)]
    out = []
    for pv, l in zip(pvs, ls):
        pv = pv / l
        c = pv.shape[0] // 2
        out.append(jnp.where(m0, pv[:c], pv[c:]))
    return out


def _pair_queries(q, m0):
    z = jnp.zeros_like(q)
    return jnp.concatenate([jnp.where(m0, q, z), jnp.where(m0, z, q)], axis=0)


def _attn_prompt_kernel(q_ref, kc_ref, kp_ref, vc_ref, vp_ref, bias_ref, y_ref, k_scr, v_scr):
    rows = q_ref.shape[1]
    n_chunks = rows // CHUNK
    t_idx = pl.program_id(1)
    k_scr[0:WINDOW, :] = kp_ref[0]
    k_scr[WINDOW:WINDOW + rows, :] = kc_ref[0]
    v_scr[0:WINDOW, :] = vp_ref[0]
    v_scr[WINDOW:WINDOW + rows, :] = vc_ref[0]
    m0 = lax.broadcasted_iota(jnp.int32, (CHUNK, LANES), 1) < HEAD_DIM
    key_i = lax.broadcasted_iota(jnp.int32, (1, BAND), 1)
    ri = lax.broadcasted_iota(jnp.int32, (2 * CHUNK, 2 * CHUNK), 0)
    rj = lax.broadcasted_iota(jnp.int32, (2 * CHUNK, 2 * CHUNK), 1)
    eye = jnp.where(ri == rj, 1.0, 0.0).astype(BF16)

    group = math.gcd(n_chunks, ATTN_GROUP_CHUNKS)

    def group_body(gi, carry, *, masked):
        units = []
        for cc in range(group):
            r0 = pl.multiple_of((gi * group + cc) * CHUNK, CHUNK)
            band = pl.ds(r0, BAND)
            for pr in range(N_PAIRS):
                ls = slice(pr * LANES, (pr + 1) * LANES)
                lhs = jnp.concatenate([_pair_queries(q_ref[0, pl.ds(r0, CHUNK), ls], m0), eye], axis=1)
                rhs = jnp.concatenate([k_scr[band, ls], bias_ref[pr]], axis=1)
                s = _dot_nt(lhs, rhs)
                if masked:
                    s = s + jnp.where(key_i + (r0 - WINDOW) >= 0, 0.0, NEG_BIG)
                units.append([(s, v_scr[band, ls])])
        outs = _softmax_pv(units, m0)
        for cc in range(group):
            r0 = pl.multiple_of((gi * group + cc) * CHUNK, CHUNK)
            for pr in range(N_PAIRS):
                y_ref[0, pl.ds(r0, CHUNK), pr * LANES:(pr + 1) * LANES] = (
                    outs[cc * N_PAIRS + pr].astype(BF16))
        return carry

    @pl.when(t_idx == 0)
    def _():
        lax.fori_loop(0, n_chunks // group, functools.partial(group_body, masked=True), 0)

    @pl.when(t_idx > 0)
    def _():
        lax.fori_loop(0, n_chunks // group, functools.partial(group_body, masked=False), 0)


def _attn_prompt(q, k, v, bias_pairs, *, rows):
    b, t, _ = q.shape
    assert rows == WINDOW
    cur = pl.BlockSpec((1, rows, D_A), lambda i, j: (i, j, 0))
    prev = pl.BlockSpec((1, rows, D_A), lambda i, j: (i, jnp.maximum(j - 1, 0), 0))
    return pl.pallas_call(
        _attn_prompt_kernel,
        grid=(b, t // rows),
        in_specs=[cur, cur, prev, cur, prev, _const_spec((N_PAIRS, BAND, LANES))],
        out_specs=cur,
        out_shape=jax.ShapeDtypeStruct((b, t, D_A), BF16),
        scratch_shapes=[pltpu.VMEM((WINDOW + rows, D_A), BF16),
                        pltpu.VMEM((WINDOW + rows, D_A), BF16)],
        compiler_params=pltpu.CompilerParams(
            dimension_semantics=("arbitrary", "arbitrary"), vmem_limit_bytes=VMEM_LIMIT),
        name="attn_prompt",
    )(q, k, k, v, v, bias_pairs)


def _attn_sample_kernel(q_ref, k_ref, v_ref, ck_ref, cv_ref, bias_c_ref, bias_n_ref, y_ref):
    rows = q_ref.shape[1]
    m0 = lax.broadcasted_iota(jnp.int32, (rows, LANES), 1) < HEAD_DIM
    units = []
    for pr in range(N_PAIRS):
        ls = slice(pr * LANES, (pr + 1) * LANES)
        lhs = _pair_queries(q_ref[0, :, ls], m0)
        s_c = _dot_nt(lhs, ck_ref[0, :, ls].astype(BF16)) + bias_c_ref[pr]
        s_n = _dot_nt(lhs, k_ref[0, :, ls]) + bias_n_ref[pr]
        units.append([(s_c, cv_ref[0, :, ls].astype(BF16)), (s_n, v_ref[0, :, ls])])
    for pr, o in enumerate(_softmax_pv(units, m0)):
        y_ref[0, :, pr * LANES:(pr + 1) * LANES] = o.astype(BF16)


def _attn_sample(q, k, v, cache_k, cache_v, bias_c, bias_n):
    b, rows, _ = q.shape
    n_cache = cache_k.shape[1]
    cur = pl.BlockSpec((1, rows, D_A), lambda i: (i, 0, 0))
    cache = pl.BlockSpec((1, n_cache, D_A), lambda i: (i, 0, 0))
    return pl.pallas_call(
        _attn_sample_kernel,
        grid=(b,),
        in_specs=[cur, cur, cur, cache, cache,
                  _const_spec((N_PAIRS, 2 * rows, n_cache)), _const_spec((N_PAIRS, 2 * rows, rows))],
        out_specs=cur,
        out_shape=jax.ShapeDtypeStruct((b, rows, D_A), BF16),
        compiler_params=pltpu.CompilerParams(dimension_semantics=("arbitrary",)),
        name="attn_sample",
    )(q, k, v, cache_k, cache_v, bias_c, bias_n)


def _rel_bias(table, n_q, n_k, offset):
    h = table.shape[0]
    period = n_q + n_k
    e = jnp.arange(period)
    e = jnp.where(e < n_k, e, e - period)
    u = table[:, jnp.clip(offset - e, -REL_CLIP, REL_CLIP) + REL_CLIP].astype(F32) * LOG2E
    toeplitz = jnp.tile(u, (1, n_q))[:, :n_q * (period - 1)].reshape(h, n_q, period - 1)
    return toeplitz[:, :, :n_k]


def _pair_rows(bias):
    h, nq, nk = bias.shape
    return bias.reshape(h // 2, 2 * nq, nk)


def _out_kernel(x_ref, yr_ref, ya_ref, gt_ref, mod_ref, wbr_ref, wba_ref, wo_ref, ng_ref,
                w1_ref, w2_ref, fg_ref, y_ref):
    nb, rows, d = x_ref.shape
    n = nb * rows
    mod = mod_ref[...]
    g1 = mod[:, :, 2 * d:3 * d]
    sh2 = mod[:, :, 3 * d:4 * d]
    sc2 = mod[:, :, 4 * d:5 * d]
    g2 = mod[:, :, 5 * d:6 * d]

    gates = gt_ref[...].reshape(n, 2 * d).astype(F32)
    y_r = yr_ref[...].reshape(n, D_R)
    y_a = ya_ref[...].reshape(n, D_A)
    merged = (jax.nn.sigmoid(gates[:, 0:d]) * _dot(y_r, wbr_ref[...])
              + jax.nn.sigmoid(gates[:, d:2 * d]) * _dot(y_a, wba_ref[...]))
    mix = _dot(merged.astype(BF16), wo_ref[...]).reshape(nb, rows, d)
    x1 = x_ref[...] + g1 * mix

    h2 = ((_rms(x1) * ng_ref[...]) * (1.0 + sc2) + sh2).reshape(n, d).astype(BF16)
    ff_blk = 1024
    acc = None
    for j in range(D_FF // ff_blk):
        mid = _dot(h2, w1_ref[:, j * ff_blk:(j + 1) * ff_blk])
        act = jnp.square(jnp.maximum(mid, 0.0)).astype(BF16)
        part = _dot(act, w2_ref[j * ff_blk:(j + 1) * ff_blk, :])
        acc = part if acc is None else acc + part
    x2 = x1 + g2 * acc.reshape(nb, rows, d)
    y_ref[...] = _rms(x2) * fg_ref[...]


def _out(x, y_r, y_a, gates, mod, w_br_r, w_br_a, w_out, norm_g, w_ff1, w_ff2, final_g, *, nb, rows):
    b, t, d = x.shape
    tile = lambda n: pl.BlockSpec((nb, rows, n), lambda i, j: (i, j, 0))
    return pl.pallas_call(
        _out_kernel,
        grid=(b // nb, t // rows),
        in_specs=[tile(d), tile(D_R), tile(D_A), tile(2 * d),
                  pl.BlockSpec((nb, 1, 6 * d), lambda i, j: (i, 0, 0)),
                  _const_spec((D_R, d)), _const_spec((D_A, d)), _const_spec((d, d)),
                  _const_spec((1, 1, d)),
                  _const_spec((d, D_FF)), _const_spec((D_FF, d)),
                  _const_spec((1, 1, d))],
        out_specs=tile(d),
        out_shape=jax.ShapeDtypeStruct((b, t, d), F32),
        compiler_params=pltpu.CompilerParams(
            dimension_semantics=("arbitrary", "arbitrary"), vmem_limit_bytes=VMEM_LIMIT),
        name="out",
    )(x, y_r, y_a, gates, mod, w_br_r, w_br_a, w_out, norm_g.reshape(1, 1, d),
      w_ff1, w_ff2, final_g.reshape(1, 1, d))


def _group(x, mod, prev0, s0, attend, W, *, nb, rows, n_tail):
    b, t, d = x.shape
    xs, q, k, v, gates, p_tail, k_tail, v_tail = _proj(
        x, mod, W["norm_mix_g"], W["w_in"], prev0, W["mu_shift"], nb=nb, rows=rows, n_tail=n_tail)
    y_r, s_pairs = _wkv(xs, _state_to_pairs(s0), W["wkv"], rows=rows)
    y_a = attend(q, k, v)
    y = _out(x, y_r, y_a, gates, mod, W["w_br_r"], W["w_br_a"], W["w_out"], W["norm_mlp_g"],
             W["w_ff1"], W["w_ff2"], W["final_norm_g"], nb=nb, rows=rows)
    return (y, _pairs_to_state(s_pairs)[None], p_tail[None, :, 7, :],
            k_tail.reshape(1, b, n_tail, N_HEADS, HEAD_DIM),
            v_tail.reshape(1, b, n_tail, N_HEADS, HEAD_DIM))


def kernel(x_prompt, x_sample, c_prompt, c_sample, state_rwkv_wkv, state_rwkv_shift, cache_att_k, cache_att_v, w_ada, b_ada, norm_mix_g, w_in, mu_shift, w_lora_up, w0, a_lora_up, a0, g_lora_up, k_k, k_a, r_k, lnx_g, lnx_b, rel_table, w_br_r, w_br_a, w_out, norm_mlp_g, w_ff1, w_ff2, final_norm_g):
    bp, tp, d = x_prompt.shape
    bs, ts, _ = x_sample.shape
    l = 0
    row = lambda a: a.reshape(1, -1)
    zl = jnp.zeros((LORA_W, D_R), F32)
    w_wa = jnp.concatenate([jnp.concatenate([w_lora_up[l], zl], axis=1),
                            jnp.concatenate([zl, a_lora_up[l]], axis=1)], axis=0).astype(BF16)
    W = dict(
        norm_mix_g=norm_mix_g[l], w_in=w_in[l].astype(BF16), mu_shift=mu_shift[l],
        wkv=(w_wa, row(w0[l]), row(a0[l]), g_lora_up[l].astype(BF16),
             row(k_k[l]), row(k_a[l]), row(r_k[l]), row(lnx_g[l]), row(lnx_b[l])),
        w_br_r=w_br_r[l].astype(BF16), w_br_a=w_br_a[l].astype(BF16), w_out=w_out[l].astype(BF16),
        norm_mlp_g=norm_mlp_g[l], w_ff1=w_ff1[l].astype(BF16), w_ff2=w_ff2[l].astype(BF16),
        final_norm_g=final_norm_g)

    mod = _ada(jnp.concatenate([c_prompt, c_sample], axis=0), w_ada[l], b_ada[l])
    mod_p = mod[:bp, None, :]
    mod_s = mod[bp:, None, :]

    bias_p = _rel_bias(rel_table[l], CHUNK, BAND, WINDOW)
    bias_t = bias_p.reshape(N_PAIRS, 2 * CHUNK, BAND).transpose(0, 2, 1).astype(BF16)
    attend_p = lambda q, k, v: _attn_prompt(q, k, v, bias_t, rows=WINDOW)
    n_keep = min(WINDOW, tp)
    y_p, p_wkv, p_shift, p_k, p_v = _group(
        x_prompt, mod_p, jnp.zeros((bp, 8, R_PROJ), F32),
        jnp.zeros((bp, N_HEADS, HEAD_DIM, HEAD_DIM), F32), attend_p, W,
        nb=1, rows=WINDOW, n_tail=n_keep)

    n_cache = cache_att_k.shape[2]
    bias_s = _pair_rows(_rel_bias(rel_table[l], ts, n_cache + ts, n_cache))
    ck = cache_att_k[l].reshape(bs, n_cache, D_A)
    cv = cache_att_v[l].reshape(bs, n_cache, D_A)
    attend_s = lambda q, k, v: _attn_sample(q, k, v, ck, cv, bias_s[:, :, :n_cache], bias_s[:, :, n_cache:])
    prev_s = jnp.broadcast_to(state_rwkv_shift[l][:, None, :], (bs, 8, R_PROJ))
    y_s, s_wkv, s_shift, s_k, s_v = _group(
        x_sample, mod_s, prev_s, state_rwkv_wkv[l], attend_s, W, nb=bs, rows=ts, n_tail=ts)

    return (y_p, y_s, p_wkv, p_shift, p_k, p_v, s_wkv, s_shift, s_k, s_v)
```

```python
import functools
import math

import jax
import jax.numpy as jnp
from jax import lax
from jax.experimental import pallas as pl
from jax.experimental.pallas import tpu as pltpu

D_MODEL = 1024
HEAD_DIM = 64
N_HEADS = 8
D_R = N_HEADS * HEAD_DIM
D_A = N_HEADS * HEAD_DIM
LORA_W = 64
LORA_A = 64
LORA_G = 128
R_PROJ = 3 * D_R + LORA_W + LORA_A + LORA_G
CHUNK = 64
BAND_CHUNKS = 8
WINDOW = BAND_CHUNKS * CHUNK
BAND = WINDOW + CHUNK
REL_CLIP = 128
D_FF = 4 * D_MODEL
IN_PROJ = R_PROJ + 3 * D_A + 2 * D_MODEL
NORM_EPS = 1e-6
GN_EPS = 64e-5
ATT_SCALE = HEAD_DIM ** -0.5
LOG2E = math.log2(math.e)
PAST_LEN = 1024

LANES = 128
N_PAIRS = N_HEADS // 2
NEG_BIG = -1e30
HEAD_SUM_BLOCK = 256
WKV_GROUP_UNITS = 16
WKV_SEQS = 2
ATTN_GROUP_CHUNKS = 4
VMEM_LIMIT = 56 * 1024 * 1024

F32 = jnp.float32
BF16 = jnp.bfloat16


def _dot(a, b):
    return lax.dot_general(a, b, (((1,), (0,)), ((), ())), preferred_element_type=F32)


def _dot_nt(a, b):
    return lax.dot_general(a, b, (((1,), (1,)), ((), ())), preferred_element_type=F32)


def _dot_tn(a, b):
    return lax.dot_general(a, b, (((0,), (0,)), ((), ())), preferred_element_type=F32)


def _head_sums(x, ones):
    xb = x.astype(BF16)
    w = ones.shape[0]
    return jnp.concatenate([_dot(xb[:, j:j + w], ones) for j in range(0, x.shape[1], w)], axis=1)


def _sigmoid(x):
    return 0.5 * jnp.tanh(0.5 * x) + 0.5


def _rms(x):
    return x * lax.rsqrt(jnp.mean(x * x, axis=-1, keepdims=True) + NORM_EPS)


def _const_spec(shape):
    nd = len(shape)
    return pl.BlockSpec(shape, lambda *_: (0,) * nd, pipeline_mode=pl.Buffered(1))


def _ada_kernel(c_ref, w_ref, b_ref, o_ref):
    c = c_ref[...]
    s = (c * jax.nn.sigmoid(c)).astype(BF16)
    o_ref[...] = _dot(s, w_ref[...].astype(BF16)) + b_ref[...]


def _ada(c, w_ada, b_ada):
    n, d = c.shape
    n_out = w_ada.shape[1]
    bn = 1024
    return pl.pallas_call(
        _ada_kernel,
        grid=(n_out // bn,),
        in_specs=[pl.BlockSpec((n, d), lambda j: (0, 0)),
                  pl.BlockSpec((d, bn), lambda j: (0, j)),
                  pl.BlockSpec((1, bn), lambda j: (0, j))],
        out_specs=pl.BlockSpec((n, bn), lambda j: (0, j)),
        out_shape=jax.ShapeDtypeStruct((n, n_out), F32),
        compiler_params=pltpu.CompilerParams(dimension_semantics=("arbitrary",)),
        name="ada",
    )(c, w_ada, b_ada.reshape(1, n_out))


def _proj_kernel(x_ref, mod_ref, g_ref, w_ref, prev0_ref, mu_ref,
                 wwa_ref, w0_ref, a0_ref, gup_ref, kk_ref, ka_ref, rk_ref, ones_ref,
                 wkv_ref, lw_ref, post_ref, q_ref, k_ref, v_ref, gt_ref, ptail_ref, ktail_ref, vtail_ref,
                 h_scr, prev_scr, *, n_tail, tiles_per_seq):
    nb, rows, d = x_ref.shape
    n = nb * rows
    step = pl.program_id(0)
    t_out = jnp.maximum(step - 1, 0) % tiles_per_seq

    @pl.when(step == 0)
    def _():
        h_scr[...] = jnp.zeros(h_scr.shape, BF16)

    @pl.when(t_out == 0)
    def _():
        prev_scr[...] = prev0_ref[...]

    mod = mod_ref[...]
    sh = mod[:, :, 0:d]
    sc = mod[:, :, d:2 * d]
    h_next = ((_rms(x_ref[...]) * g_ref[...]) * (1.0 + sc) + sh).reshape(nb * rows, d).astype(BF16)

    def col(lo, hi):
        return _dot(h_scr[...], w_ref[:, lo:hi])

    first_row = lax.broadcasted_iota(jnp.int32, (1, rows, 1), 1) == 0

    def shift_mix(lo, hi, p2):
        cs = slice(lo, hi)
        p = p2.reshape(nb, rows, hi - lo)
        shifted = pltpu.roll(p2, 1, axis=0).reshape(nb, rows, hi - lo)
        p_prev = jnp.where(first_row, prev_scr[:, 0:1, cs], shifted)
        prev_scr[:, 0:1, cs] = p[:, rows - 1:rows, :]
        ptail_ref[:, :, cs] = p[:, rows - 8:, :]
        return (p + (p_prev - p) * mu_ref[:, :, cs]).reshape(n, hi - lo)

    def put(ref, j, val):
        ref[:, :, j * D_R:(j + 1) * D_R] = val.reshape(nb, rows, D_R).astype(ref.dtype)

    mixed = {}

    def finish_r(p2):
        mixed["r"] = shift_mix(0, D_R, p2)
        put(wkv_ref, 0, mixed["r"])

    def finish_k(p2):
        mixed["k"] = shift_mix(D_R, 2 * D_R, p2)

    def finish_rest(p2):
        xs = shift_mix(2 * D_R, R_PROJ, p2)
        r, k, v = mixed["r"], mixed["k"], xs[:, 0:D_R]
        x_wa = xs[:, D_R:D_R + LANES]
        x_g = xs[:, D_R + LANES:D_R + 2 * LANES]
        lane = lax.broadcasted_iota(jnp.int32, (n, LANES), 1)
        wa = _dot(jnp.where(lane < LORA_W, jnp.tanh(x_wa), x_wa).astype(BF16), wwa_ref[...])
        lw = -math.exp(-0.5) * _sigmoid(wa[:, 0:D_R] + w0_ref[...])
        a = _sigmoid(wa[:, D_R:2 * D_R] + a0_ref[...])
        g = _dot(_sigmoid(x_g).astype(BF16), gup_ref[...])
        head_ones = ones_ref[...]
        kk = k * kk_ref[...]
        kk = kk * lax.rsqrt(jnp.maximum(_head_sums(kk * kk, head_ones), 1e-24))
        k2 = k * (1.0 + (a - 1.0) * ka_ref[...])
        bonus = _head_sums(r * k2 * rk_ref[...], head_ones) * v
        put(wkv_ref, 1, k2)
        put(wkv_ref, 2, v)
        put(wkv_ref, 3, kk)
        put(wkv_ref, 4, a * kk)
        lw_ref[...] = lw.reshape(nb, rows, D_R)
        put(post_ref, 0, g)
        put(post_ref, 1, bonus)

    def plain(ref, cs, scale=None, tail=None):
        def finish(t2):
            t = t2.reshape(nb, rows, t2.shape[1])
            ref[:, :, cs] = (t if scale is None else t * scale).astype(BF16)
            if tail is not None:
                tail[...] = t[:, rows - n_tail:, :]
        return finish

    g0 = R_PROJ + 3 * D_A
    full = slice(None)
    jobs = [
        (0, D_R, finish_r),
        (D_R, 2 * D_R, finish_k),
        (2 * D_R, R_PROJ, finish_rest),
        (R_PROJ, R_PROJ + D_A, plain(q_ref, full, scale=ATT_SCALE * LOG2E)),
        (R_PROJ + D_A, R_PROJ + 2 * D_A, plain(k_ref, full, tail=ktail_ref)),
        (R_PROJ + 2 * D_A, g0, plain(v_ref, full, tail=vtail_ref)),
        (g0, g0 + d, plain(gt_ref, slice(0, d))),
        (g0 + d, g0 + 2 * d, plain(gt_ref, slice(d, 2 * d))),
    ]
    pending = None
    for lo, hi, finish in jobs:
        res = col(lo, hi)
        if pending is not None:
            pending[0](pending[1])
        pending = (finish, res)
    pending[0](pending[1])

    h_scr[...] = h_next


def _proj(x, mod, norm_g, w_in, prev0, mu, wp, *, nb, rows, n_tail):
    b, t, d = x.shape
    vec = _const_spec((1, D_R))
    tiles_per_seq = t // rows
    n_tiles = (b // nb) * tiles_per_seq
    t_in = lambda s: jnp.minimum(s, n_tiles - 1)
    t_out = lambda s: jnp.maximum(s - 1, 0)
    tile_in = lambda n: pl.BlockSpec(
        (nb, rows, n), lambda s: (t_in(s) // tiles_per_seq, t_in(s) % tiles_per_seq, 0))
    tile = lambda n: pl.BlockSpec(
        (nb, rows, n), lambda s: (t_out(s) // tiles_per_seq, t_out(s) % tiles_per_seq, 0))
    tail = lambda r, n: pl.BlockSpec((nb, r, n), lambda s: (t_out(s) // tiles_per_seq, 0, 0))
    out_shape = (
        jax.ShapeDtypeStruct((b, t, 5 * D_R), BF16),
        jax.ShapeDtypeStruct((b, t, D_R), F32),
        jax.ShapeDtypeStruct((b, t, 2 * D_R), BF16),
        jax.ShapeDtypeStruct((b, t, D_A), BF16),
        jax.ShapeDtypeStruct((b, t, D_A), BF16),
        jax.ShapeDtypeStruct((b, t, D_A), BF16),
        jax.ShapeDtypeStruct((b, t, 2 * d), BF16),
        jax.ShapeDtypeStruct((b, 8, R_PROJ), F32),
        jax.ShapeDtypeStruct((b, n_tail, D_A), F32),
        jax.ShapeDtypeStruct((b, n_tail, D_A), F32),
    )
    return pl.pallas_call(
        functools.partial(_proj_kernel, n_tail=n_tail, tiles_per_seq=tiles_per_seq),
        grid=(n_tiles + 1,),
        in_specs=[tile_in(d),
                  pl.BlockSpec((nb, 1, 6 * d), lambda s: (t_in(s) // tiles_per_seq, 0, 0)),
                  _const_spec((1, 1, d)),
                  _const_spec((d, IN_PROJ)),
                  tail(8, R_PROJ),
                  _const_spec((1, 1, R_PROJ)),
                  _const_spec((LANES, 2 * D_R)), vec, vec, _const_spec((LORA_G, D_R)), vec, vec, vec,
                  _const_spec((HEAD_SUM_BLOCK, HEAD_SUM_BLOCK))],
        out_specs=(tile(5 * D_R), tile(D_R), tile(2 * D_R), tile(D_A), tile(D_A), tile(D_A), tile(2 * d),
                   tail(8, R_PROJ), tail(n_tail, D_A), tail(n_tail, D_A)),
        out_shape=out_shape,
        scratch_shapes=[pltpu.VMEM((nb * rows, d), BF16),
                        pltpu.VMEM((nb, 8, R_PROJ), F32)],
        compiler_params=pltpu.CompilerParams(
            dimension_semantics=("arbitrary",), vmem_limit_bytes=VMEM_LIMIT),
        name="proj",
    )(x, mod, norm_g.reshape(1, 1, d), w_in, prev0, mu.reshape(1, 1, R_PROJ), *wp)


def _pair_blocks(x, m0):
    xb = x.astype(BF16)
    z = jnp.zeros_like(xb)
    return jnp.concatenate([jnp.where(m0, xb, z), jnp.where(m0, z, xb)], axis=0)


def _wkv_chunk_local(units, masks):
    m0, strict, incl, blockdiag = masks
    c = CHUNK
    eye = jnp.where(incl, 1.0, 0.0) - jnp.where(strict, 1.0, 0.0)
    zero = jnp.zeros((c, LANES), F32)
    zero2 = jnp.zeros((LANES, LANES), F32)
    pre = []
    for r, k, v, kk, b, lw, cum in units:
        r_t = r * jnp.exp(cum)
        kk_t = kk * jnp.exp(cum - lw)
        e_inv = jnp.exp(-cum)
        cum_last = cum[c - 1:c, :]
        dec = jnp.exp(cum_last - cum)
        pre.append(dict(
            r_t=r_t, kk_t=kk_t, v=v, lhs=jnp.concatenate([kk_t, r_t], axis=0).astype(BF16),
            k_h=_pair_blocks(k * e_inv, m0), b_h=_pair_blocks(b * e_inv, m0),
            v_blk=_pair_blocks(v, m0), k_p=k * dec, b_p=b * dec, p_last=jnp.exp(cum_last)))
    g1 = [_dot_nt(p["lhs"], p["k_h"]) for p in pre]
    g2 = [_dot_nt(p["lhs"], p["b_h"]) for p in pre]
    m_k = [jnp.where(strict, g[:c], zero).astype(BF16) for g in g1]
    a_rk = [jnp.where(incl, g[c:], zero) for g in g1]
    n_pow = [jnp.where(strict, g[:c], zero) for g in g2]
    a_rb = [jnp.where(incl, g[c:], zero) for g in g2]
    m_kv = [_dot(m, p["v_blk"]) for m, p in zip(m_k, pre)]

    t_inv = [eye - n for n in n_pow]
    n_pow = [_dot(n.astype(BF16), _pair_blocks(n, m0)) for n in n_pow]
    n_levels = int(math.log2(c)) - 1
    for level in range(n_levels):
        n_blk = [_pair_blocks(n, m0) for n in n_pow]
        if level == n_levels - 1:
            t_inv = [t + _dot(t.astype(BF16), nb) for t, nb in zip(t_inv, n_blk)]
        else:
            prod = [_dot(jnp.concatenate([n, t], axis=0).astype(BF16), nb)
                    for n, t, nb in zip(n_pow, t_inv, n_blk)]
            n_pow = [pq[:c] for pq in prod]
            t_inv = [t + pq[c:] for t, pq in zip(t_inv, prod)]
    y = [_dot(t.astype(BF16),
              jnp.concatenate([_pair_blocks(mv, m0), _pair_blocks(p["kk_t"], m0)], axis=1))
         for t, mv, p in zip(t_inv, m_kv, pre)]
    u_loc = [yi[:, :LANES] for yi in y]
    q = [yi[:, LANES:] for yi in y]

    o_loc = [_dot(jnp.concatenate([ak, ab], axis=1).astype(BF16),
                  jnp.concatenate([p["v_blk"], _pair_blocks(-u, m0)], axis=0))
             for ak, ab, u, p in zip(a_rk, a_rb, u_loc, pre)]
    r_eff = [p["r_t"] - _dot(ab.astype(BF16), _pair_blocks(qi, m0)) for p, ab, qi in zip(pre, a_rb, q)]
    d_s = [jnp.where(blockdiag,
                     _dot_tn(jnp.concatenate([p["v"], u], axis=0).astype(BF16),
                             jnp.concatenate([p["k_p"], -p["b_p"]], axis=0).astype(BF16)), zero2)
           for p, u in zip(pre, u_loc)]
    qtb = [jnp.where(blockdiag, _dot_tn(qi.astype(BF16), p["b_p"].astype(BF16)), zero2)
           for p, qi in zip(pre, q)]
    return [(re, ol, qb, ds, p["p_last"]) for re, ol, qb, ds, p in zip(r_eff, o_loc, qtb, d_s, pre)]


def _wkv_kernel(x_ref, lw_ref, s0_ref, o_ref, sout_ref,
                s_scr, oloc_scr, reff_scr, qtb_scr, ds_scr, pc_scr):
    n_seq, rows, _ = x_ref.shape
    n_chunks = rows // CHUNK
    t_idx = pl.program_id(1)

    @pl.when(t_idx == 0)
    def _():
        s_scr[...] = s0_ref[...]

    ci = lax.broadcasted_iota(jnp.int32, (CHUNK, CHUNK), 0)
    cj = lax.broadcasted_iota(jnp.int32, (CHUNK, CHUNK), 1)
    tri = jnp.where(cj <= ci, 1.0, 0.0).astype(BF16)
    lane_c = lax.broadcasted_iota(jnp.int32, (CHUNK, LANES), 1)
    row_c = lax.broadcasted_iota(jnp.int32, (CHUNK, LANES), 0)
    m0 = lane_c < HEAD_DIM
    jj = jnp.where(m0, lane_c, lane_c - HEAD_DIM)
    br = lax.broadcasted_iota(jnp.int32, (LANES, LANES), 0) < HEAD_DIM
    bc = lax.broadcasted_iota(jnp.int32, (LANES, LANES), 1) < HEAD_DIM
    masks = (m0, jj < row_c, jj <= row_c, br == bc)

    group = math.gcd(n_chunks, max(1, WKV_GROUP_UNITS // (n_seq * N_PAIRS)))

    def unit_index(si, chunk, pr):
        return (si * n_chunks + chunk) * N_PAIRS + pr

    def local_body(gi, carry):
        units = []
        for si in range(n_seq):
            for cc in range(group):
                sl = pl.ds(pl.multiple_of((gi * group + cc) * CHUNK, CHUNK), CHUNK)
                lw_c = lw_ref[si, sl, :]
                cum = _split_dot_lhs(tri, lw_c)
                for pr in range(N_PAIRS):
                    ls = slice(pr * LANES, (pr + 1) * LANES)
                    cols = [x_ref[si, sl, j * D_R + pr * LANES:j * D_R + (pr + 1) * LANES].astype(F32)
                            for j in range(5)]
                    units.append((*cols, lw_c[:, ls], cum[:, ls]))
        res = iter(_wkv_chunk_local(units, masks))
        for si in range(n_seq):
            for cc in range(group):
                chunk = gi * group + cc
                sl = pl.ds(pl.multiple_of(chunk * CHUNK, CHUNK), CHUNK)
                for pr in range(N_PAIRS):
                    r_eff, o_loc, qtb, d_s, p_last = next(res)
                    u = unit_index(si, chunk, pr)
                    reff_scr[u] = r_eff.astype(BF16)
                    oloc_scr[si, sl, pr * LANES:(pr + 1) * LANES] = o_loc
                    qtb_scr[u] = qtb.astype(BF16)
                    ds_scr[u] = d_s
                    pc_scr[u] = jnp.broadcast_to(p_last, (8, LANES))
        return carry

    lax.fori_loop(0, n_chunks // group, local_body, 0)

    chains = [(si, pr) for si in range(n_seq) for pr in range(N_PAIRS)]

    def state_body(chunk, carry):
        sl = pl.ds(pl.multiple_of(chunk * CHUNK, CHUNK), CHUNK)
        s_old = [s_scr[si, pr] for si, pr in chains]
        s_bf = [s.astype(BF16) for s in s_old]
        us = [unit_index(si, chunk, pr) for si, pr in chains]
        o_add = [_dot_nt(reff_scr[u], sb) for u, sb in zip(us, s_bf)]
        s_mix = [_dot(sb, qtb_scr[u]) for u, sb in zip(us, s_bf)]
        for i, (si, pr) in enumerate(chains):
            ls = slice(pr * LANES, (pr + 1) * LANES)
            o_ref[si, sl, ls] = (oloc_scr[si, sl, ls] + o_add[i]).astype(BF16)
            s_scr[si, pr] = s_old[i] * pc_scr[us[i]][0:1, :] + ds_scr[us[i]] - s_mix[i]
        return carry

    lax.fori_loop(0, n_chunks, state_body, 0)

    @pl.when(t_idx == pl.num_programs(1) - 1)
    def _():
        sout_ref[...] = s_scr[...]


def _split_dot_lhs(tri, x):
    hi = x.astype(BF16)
    r1 = x - hi.astype(F32)
    mid = r1.astype(BF16)
    lo = (r1 - mid.astype(F32)).astype(BF16)
    return _dot(tri, hi) + _dot(tri, mid) + _dot(tri, lo)


def _wkv(x, lw, s0_pairs, *, rows, n_seq):
    b, t, _ = x.shape
    n_units = n_seq * rows // CHUNK * N_PAIRS
    tile = lambda n: pl.BlockSpec((n_seq, rows, n), lambda i, j: (i, j, 0))
    state = pl.BlockSpec((n_seq, N_PAIRS, LANES, LANES), lambda i, j: (i, 0, 0, 0))
    return pl.pallas_call(
        _wkv_kernel,
        grid=(b // n_seq, t // rows),
        in_specs=[tile(5 * D_R), tile(D_R), state],
        out_specs=(tile(D_R), state),
        out_shape=(jax.ShapeDtypeStruct((b, t, D_R), BF16),
                   jax.ShapeDtypeStruct((b, N_PAIRS, LANES, LANES), F32)),
        scratch_shapes=[pltpu.VMEM((n_seq, N_PAIRS, LANES, LANES), F32),
                        pltpu.VMEM((n_seq, rows, D_R), F32),
                        pltpu.VMEM((n_units, CHUNK, LANES), BF16),
                        pltpu.VMEM((n_units, LANES, LANES), BF16),
                        pltpu.VMEM((n_units, LANES, LANES), F32),
                        pltpu.VMEM((n_units, 8, LANES), F32)],
        compiler_params=pltpu.CompilerParams(
            dimension_semantics=("arbitrary", "arbitrary"), vmem_limit_bytes=VMEM_LIMIT),
        name="wkv",
    )(x, lw, s0_pairs)


def _state_to_pairs(s):
    b = s.shape[0]
    s = s.reshape(b, N_PAIRS, 2, HEAD_DIM, HEAD_DIM)
    z = jnp.zeros_like(s[:, :, 0])
    top = jnp.concatenate([s[:, :, 0], z], axis=-1)
    bot = jnp.concatenate([z, s[:, :, 1]], axis=-1)
    return jnp.concatenate([top, bot], axis=-2)


def _pairs_to_state(sp):
    b = sp.shape[0]
    h0 = sp[:, :, :HEAD_DIM, :HEAD_DIM]
    h1 = sp[:, :, HEAD_DIM:, HEAD_DIM:]
    return jnp.stack([h0, h1], axis=2).reshape(b, N_HEADS, HEAD_DIM, HEAD_DIM)


def _softmax_pv(units, m0):
    mx = []
    for parts in units:
        m = None
        for s, _ in parts:
            mi = jnp.max(s, axis=-1, keepdims=True)
            m = mi if m is None else jnp.maximum(m, mi)
        mx.append(m)
    es = [[jnp.exp2(s - m) for s, _ in parts] for parts, m in zip(units, mx)]
    ls = [functools.reduce(lambda a, b: a + b, [jnp.sum(e, axis=-1, keepdims=True) for e in ep])
          for ep in es]
    pvs = [functools.reduce(lambda a, b: a + b,
                            [_dot(e.astype(BF16), vals) for e, (_, vals) in zip(ep, parts)])
           for ep, parts in zip(es, units)]
    out = []
    for pv, l in zip(pvs, ls):
        pv = pv / l
        c = pv.shape[0] // 2
        out.append(jnp.where(m0, pv[:c], pv[c:]))
    return out


def _pair_queries(q, m0):
    z = jnp.zeros_like(q)
    return jnp.concatenate([jnp.where(m0, q, z), jnp.where(m0, z, q)], axis=0)


def _attn_prompt_kernel(q_ref, kc_ref, kp_ref, vc_ref, vp_ref, bias_ref, y_ref, k_scr, v_scr):
    rows = q_ref.shape[1]
    n_chunks = rows // CHUNK
    t_idx = pl.program_id(1)
    k_scr[0:WINDOW, :] = kp_ref[0]
    k_scr[WINDOW:WINDOW + rows, :] = kc_ref[0]
    v_scr[0:WINDOW, :] = vp_ref[0]
    v_scr[WINDOW:WINDOW + rows, :] = vc_ref[0]
    m0 = lax.broadcasted_iota(jnp.int32, (CHUNK, LANES), 1) < HEAD_DIM
    key_i = lax.broadcasted_iota(jnp.int32, (1, BAND), 1)
    ri = lax.broadcasted_iota(jnp.int32, (2 * CHUNK, 2 * CHUNK), 0)
    rj = lax.broadcasted_iota(jnp.int32, (2 * CHUNK, 2 * CHUNK), 1)
    eye = jnp.where(ri == rj, 1.0, 0.0).astype(BF16)

    group = math.gcd(n_chunks, ATTN_GROUP_CHUNKS)

    def group_body(gi, carry, *, masked):
        units = []
        for cc in range(group):
            r0 = pl.multiple_of((gi * group + cc) * CHUNK, CHUNK)
            band = pl.ds(r0, BAND)
            for pr in range(N_PAIRS):
                ls = slice(pr * LANES, (pr + 1) * LANES)
                lhs = jnp.concatenate([_pair_queries(q_ref[0, pl.ds(r0, CHUNK), ls], m0), eye], axis=1)
                rhs = jnp.concatenate([k_scr[band, ls], bias_ref[pr]], axis=1)
                s = _dot_nt(lhs, rhs)
                if masked:
                    s = s + jnp.where(key_i + (r0 - WINDOW) >= 0, 0.0, NEG_BIG)
                units.append([(s, v_scr[band, ls])])
        outs = _softmax_pv(units, m0)
        for cc in range(group):
            r0 = pl.multiple_of((gi * group + cc) * CHUNK, CHUNK)
            for pr in range(N_PAIRS):
                y_ref[0, pl.ds(r0, CHUNK), pr * LANES:(pr + 1) * LANES] = (
                    outs[cc * N_PAIRS + pr].astype(BF16))
        return carry

    @pl.when(t_idx == 0)
    def _():
        lax.fori_loop(0, n_chunks // group, functools.partial(group_body, masked=True), 0)

    @pl.when(t_idx > 0)
    def _():
        lax.fori_loop(0, n_chunks // group, functools.partial(group_body, masked=False), 0)


def _attn_prompt(q, k, v, bias_pairs, *, rows):
    b, t, _ = q.shape
    assert rows == WINDOW
    cur = pl.BlockSpec((1, rows, D_A), lambda i, j: (i, j, 0))
    prev = pl.BlockSpec((1, rows, D_A), lambda i, j: (i, jnp.maximum(j - 1, 0), 0))
    return pl.pallas_call(
        _attn_prompt_kernel,
        grid=(b, t // rows),
        in_specs=[cur, cur, prev, cur, prev, _const_spec((N_PAIRS, BAND, LANES))],
        out_specs=cur,
        out_shape=jax.ShapeDtypeStruct((b, t, D_A), BF16),
        scratch_shapes=[pltpu.VMEM((WINDOW + rows, D_A), BF16),
                        pltpu.VMEM((WINDOW + rows, D_A), BF16)],
        compiler_params=pltpu.CompilerParams(
            dimension_semantics=("arbitrary", "arbitrary"), vmem_limit_bytes=VMEM_LIMIT),
        name="attn_prompt",
    )(q, k, k, v, v, bias_pairs)


def _attn_sample_kernel(q_ref, k_ref, v_ref, ck_ref, cv_ref, bias_c_ref, bias_n_ref, y_ref):
    rows = q_ref.shape[1]
    m0 = lax.broadcasted_iota(jnp.int32, (rows, LANES), 1) < HEAD_DIM
    units = []
    for pr in range(N_PAIRS):
        ls = slice(pr * LANES, (pr + 1) * LANES)
        lhs = _pair_queries(q_ref[0, :, ls], m0)
        s_c = _dot_nt(lhs, ck_ref[0, :, ls].astype(BF16)) + bias_c_ref[pr]
        s_n = _dot_nt(lhs, k_ref[0, :, ls]) + bias_n_ref[pr]
        units.append([(s_c, cv_ref[0, :, ls].astype(BF16)), (s_n, v_ref[0, :, ls])])
    for pr, o in enumerate(_softmax_pv(units, m0)):
        y_ref[0, :, pr * LANES:(pr + 1) * LANES] = o.astype(BF16)


def _attn_sample(q, k, v, cache_k, cache_v, bias_c, bias_n):
    b, rows, _ = q.shape
    n_cache = cache_k.shape[1]
    cur = pl.BlockSpec((1, rows, D_A), lambda i: (i, 0, 0))
    cache = pl.BlockSpec((1, n_cache, D_A), lambda i: (i, 0, 0))
    return pl.pallas_call(
        _attn_sample_kernel,
        grid=(b,),
        in_specs=[cur, cur, cur, cache, cache,
                  _const_spec((N_PAIRS, 2 * rows, n_cache)), _const_spec((N_PAIRS, 2 * rows, rows))],
        out_specs=cur,
        out_shape=jax.ShapeDtypeStruct((b, rows, D_A), BF16),
        compiler_params=pltpu.CompilerParams(dimension_semantics=("arbitrary",)),
        name="attn_sample",
    )(q, k, v, cache_k, cache_v, bias_c, bias_n)


def _rel_bias(table, n_q, n_k, offset):
    h = table.shape[0]
    period = n_q + n_k
    e = jnp.arange(period)
    e = jnp.where(e < n_k, e, e - period)
    u = table[:, jnp.clip(offset - e, -REL_CLIP, REL_CLIP) + REL_CLIP].astype(F32) * LOG2E
    toeplitz = jnp.tile(u, (1, n_q))[:, :n_q * (period - 1)].reshape(h, n_q, period - 1)
    return toeplitz[:, :, :n_k]


def _pair_rows(bias):
    h, nq, nk = bias.shape
    return bias.reshape(h // 2, 2 * nq, nk)


def _out_kernel(x_ref, o_ref, post_ref, ya_ref, gt_ref, mod_ref, lng_ref, lnb_ref, ones_ref,
                wbr_ref, wba_ref, wo_ref, ng_ref, w1_ref, w2_ref, fg_ref, y_ref):
    nb, rows, d = x_ref.shape
    n = nb * rows
    mod = mod_ref[...]
    g1 = mod[:, :, 2 * d:3 * d]
    sh2 = mod[:, :, 3 * d:4 * d]
    sc2 = mod[:, :, 4 * d:5 * d]
    g2 = mod[:, :, 5 * d:6 * d]

    gates = gt_ref[...].reshape(n, 2 * d).astype(F32)
    branch_a = _sigmoid(gates[:, d:2 * d]) * _dot(ya_ref[...].reshape(n, D_A), wba_ref[...])

    head_ones = ones_ref[...]
    o = o_ref[...].reshape(n, D_R)
    post = post_ref[...].reshape(n, 2 * D_R).astype(F32)
    inv_n = 1.0 / HEAD_DIM
    mean = _head_sums(o, head_ones) * inv_n
    cen = o.astype(F32) - mean
    var = _head_sums(cen * cen, head_ones) * inv_n
    o_n = cen * lax.rsqrt(var + GN_EPS) * lng_ref[...] + lnb_ref[...]
    y_r = ((o_n + post[:, D_R:2 * D_R]) * post[:, 0:D_R]).astype(BF16)

    merged = _sigmoid(gates[:, 0:d]) * _dot(y_r, wbr_ref[...]) + branch_a
    mix = _dot(merged.astype(BF16), wo_ref[...]).reshape(nb, rows, d)
    x1 = x_ref[...] + g1 * mix

    h2 = ((_rms(x1) * ng_ref[...]) * (1.0 + sc2) + sh2).reshape(n, d).astype(BF16)
    ff_blk = 1024
    acc = None
    for j in range(D_FF // ff_blk):
        mid = _dot(h2, w1_ref[:, j * ff_blk:(j + 1) * ff_blk])
        act = jnp.square(jnp.maximum(mid, 0.0)).astype(BF16)
        part = _dot(act, w2_ref[j * ff_blk:(j + 1) * ff_blk, :])
        acc = part if acc is None else acc + part
    x2 = x1 + g2 * acc.reshape(nb, rows, d)
    y_ref[...] = _rms(x2) * fg_ref[...]


def _out(x, o, post, y_a, gates, mod, lnx_g, lnx_b, head_ones,
         w_br_r, w_br_a, w_out, norm_g, w_ff1, w_ff2, final_g, *, nb, rows):
    b, t, d = x.shape
    tile = lambda n: pl.BlockSpec((nb, rows, n), lambda i, j: (i, j, 0))
    return pl.pallas_call(
        _out_kernel,
        grid=(b // nb, t // rows),
        in_specs=[tile(d), tile(D_R), tile(2 * D_R), tile(D_A), tile(2 * d),
                  pl.BlockSpec((nb, 1, 6 * d), lambda i, j: (i, 0, 0)),
                  _const_spec((1, D_R)), _const_spec((1, D_R)), _const_spec((HEAD_SUM_BLOCK, HEAD_SUM_BLOCK)),
                  _const_spec((D_R, d)), _const_spec((D_A, d)), _const_spec((d, d)),
                  _const_spec((1, 1, d)),
                  _const_spec((d, D_FF)), _const_spec((D_FF, d)),
                  _const_spec((1, 1, d))],
        out_specs=tile(d),
        out_shape=jax.ShapeDtypeStruct((b, t, d), F32),
        compiler_params=pltpu.CompilerParams(
            dimension_semantics=("arbitrary", "arbitrary"), vmem_limit_bytes=VMEM_LIMIT),
        name="out",
    )(x, o, post, y_a, gates, mod, lnx_g, lnx_b, head_ones,
      w_br_r, w_br_a, w_out, norm_g.reshape(1, 1, d), w_ff1, w_ff2, final_g.reshape(1, 1, d))


def _group(x, mod, prev0, s0, attend, W, *, nb, rows, n_tail):
    b, t, d = x.shape
    wkv_in, lw, post, q, k, v, gates, p_tail, k_tail, v_tail = _proj(
        x, mod, W["norm_mix_g"], W["w_in"], prev0, W["mu_shift"], W["rwkv_in"],
        nb=nb, rows=rows, n_tail=n_tail)
    t_pad = -(-t // CHUNK) * CHUNK
    pad = lambda a: jnp.pad(a, ((0, 0), (0, t_pad - t), (0, 0)))
    o, s_pairs = _wkv(pad(wkv_in), pad(lw), _state_to_pairs(s0),
                      rows=min(t_pad, WINDOW), n_seq=math.gcd(b, WKV_SEQS))
    y_a = attend(q, k, v)
    y = _out(x, o[:, :t], post, y_a, gates, mod, W["lnx_g"], W["lnx_b"], W["rwkv_in"][-1],
             W["w_br_r"], W["w_br_a"], W["w_out"], W["norm_mlp_g"],
             W["w_ff1"], W["w_ff2"], W["final_norm_g"], nb=nb, rows=rows)
    return (y, _pairs_to_state(s_pairs)[None], p_tail[None, :, 7, :],
            k_tail.reshape(1, b, n_tail, N_HEADS, HEAD_DIM),
            v_tail.reshape(1, b, n_tail, N_HEADS, HEAD_DIM))


def kernel(x_prompt, x_sample, c_prompt, c_sample, state_rwkv_wkv, state_rwkv_shift, cache_att_k, cache_att_v, w_ada, b_ada, norm_mix_g, w_in, mu_shift, w_lora_up, w0, a_lora_up, a0, g_lora_up, k_k, k_a, r_k, lnx_g, lnx_b, rel_table, w_br_r, w_br_a, w_out, norm_mlp_g, w_ff1, w_ff2, final_norm_g):
    bp, tp, d = x_prompt.shape
    bs, ts, _ = x_sample.shape
    l = 0
    row = lambda a: a.reshape(1, -1)
    zl = jnp.zeros((LORA_W, D_R), F32)
    w_wa = jnp.concatenate([jnp.concatenate([w_lora_up[l], zl], axis=1),
                            jnp.concatenate([zl, a_lora_up[l]], axis=1)], axis=0).astype(BF16)
    head_id = jnp.arange(HEAD_SUM_BLOCK) // HEAD_DIM
    head_ones = (head_id[:, None] == head_id[None, :]).astype(BF16)
    W = dict(
        norm_mix_g=norm_mix_g[l], w_in=w_in[l].astype(BF16), mu_shift=mu_shift[l],
        rwkv_in=(w_wa, row(w0[l]), row(a0[l]), g_lora_up[l].astype(BF16),
                 row(k_k[l]), row(k_a[l]), row(r_k[l]), head_ones),
        lnx_g=row(lnx_g[l]), lnx_b=row(lnx_b[l]),
        w_br_r=w_br_r[l].astype(BF16), w_br_a=w_br_a[l].astype(BF16), w_out=w_out[l].astype(BF16),
        norm_mlp_g=norm_mlp_g[l], w_ff1=w_ff1[l].astype(BF16), w_ff2=w_ff2[l].astype(BF16),
        final_norm_g=final_norm_g)

    mod = _ada(jnp.concatenate([c_prompt, c_sample], axis=0), w_ada[l], b_ada[l])
    mod_p = mod[:bp, None, :]
    mod_s = mod[bp:, None, :]

    bias_p = _rel_bias(rel_table[l], CHUNK, BAND, WINDOW)
    bias_t = bias_p.reshape(N_PAIRS, 2 * CHUNK, BAND).transpose(0, 2, 1).astype(BF16)
    attend_p = lambda q, k, v: _attn_prompt(q, k, v, bias_t, rows=WINDOW)
    n_keep = min(WINDOW, tp)
    y_p, p_wkv, p_shift, p_k, p_v = _group(
        x_prompt, mod_p, jnp.zeros((bp, 8, R_PROJ), F32),
        jnp.zeros((bp, N_HEADS, HEAD_DIM, HEAD_DIM), F32), attend_p, W,
        nb=1, rows=WINDOW, n_tail=n_keep)

    n_cache = cache_att_k.shape[2]
    bias_s = _pair_rows(_rel_bias(rel_table[l], ts, n_cache + ts, n_cache))
    ck = cache_att_k[l].reshape(bs, n_cache, D_A)
    cv = cache_att_v[l].reshape(bs, n_cache, D_A)
    attend_s = lambda q, k, v: _attn_sample(q, k, v, ck, cv, bias_s[:, :, :n_cache], bias_s[:, :, n_cache:])
    prev_s = jnp.broadcast_to(state_rwkv_shift[l][:, None, :], (bs, 8, R_PROJ))
    y_s, s_wkv, s_shift, s_k, s_v = _group(
        x_sample, mod_s, prev_s, state_rwkv_wkv[l], attend_s, W, nb=bs, rows=ts, n_tail=ts)

    return (y_p, y_s, p_wkv, p_shift, p_k, p_v, s_wkv, s_shift, s_k, s_v)
```

```python
import functools
import math

import jax
import jax.numpy as jnp
from jax import lax
from jax.experimental import pallas as pl
from jax.experimental.pallas import tpu as pltpu

D_MODEL = 1024
HEAD_DIM = 64
N_HEADS = 8
D_R = N_HEADS * HEAD_DIM
D_A = N_HEADS * HEAD_DIM
LORA_W = 64
LORA_A = 64
LORA_G = 128
R_PROJ = 3 * D_R + LORA_W + LORA_A + LORA_G
CHUNK = 64
BAND_CHUNKS = 8
WINDOW = BAND_CHUNKS * CHUNK
BAND = WINDOW + CHUNK
REL_CLIP = 128
D_FF = 4 * D_MODEL
IN_PROJ = R_PROJ + 3 * D_A + 2 * D_MODEL
NORM_EPS = 1e-6
GN_EPS = 64e-5
ATT_SCALE = HEAD_DIM ** -0.5
LOG2E = math.log2(math.e)
PAST_LEN = 1024

LANES = 128
N_PAIRS = N_HEADS // 2
NEG_BIG = -1e30
HEAD_SUM_BLOCK = 256
WKV_GROUP_UNITS = 16
WKV_SEQS = 4
WKV_ROWS = 256
ATTN_GROUP_CHUNKS = 4
VMEM_LIMIT = 56 * 1024 * 1024

F32 = jnp.float32
BF16 = jnp.bfloat16


def _dot(a, b):
    return lax.dot_general(a, b, (((1,), (0,)), ((), ())), preferred_element_type=F32)


def _dot_nt(a, b):
    return lax.dot_general(a, b, (((1,), (1,)), ((), ())), preferred_element_type=F32)


def _dot_tn(a, b):
    return lax.dot_general(a, b, (((0,), (0,)), ((), ())), preferred_element_type=F32)


def _head_sums(x, ones):
    xb = x.astype(BF16)
    w = ones.shape[0]
    return jnp.concatenate([_dot(xb[:, j:j + w], ones) for j in range(0, x.shape[1], w)], axis=1)


def _sigmoid(x):
    return 0.5 * jnp.tanh(0.5 * x) + 0.5


def _rms(x):
    return x * lax.rsqrt(jnp.mean(x * x, axis=-1, keepdims=True) + NORM_EPS)


def _const_spec(shape):
    nd = len(shape)
    return pl.BlockSpec(shape, lambda *_: (0,) * nd, pipeline_mode=pl.Buffered(1))


def _ada_kernel(c_ref, w_ref, b_ref, o_ref):
    c = c_ref[...]
    s = (c * jax.nn.sigmoid(c)).astype(BF16)
    o_ref[...] = _dot(s, w_ref[...].astype(BF16)) + b_ref[...]


def _ada(c, w_ada, b_ada):
    n, d = c.shape
    n_out = w_ada.shape[1]
    bn = 1024
    return pl.pallas_call(
        _ada_kernel,
        grid=(n_out // bn,),
        in_specs=[pl.BlockSpec((n, d), lambda j: (0, 0)),
                  pl.BlockSpec((d, bn), lambda j: (0, j)),
                  pl.BlockSpec((1, bn), lambda j: (0, j))],
        out_specs=pl.BlockSpec((n, bn), lambda j: (0, j)),
        out_shape=jax.ShapeDtypeStruct((n, n_out), F32),
        compiler_params=pltpu.CompilerParams(dimension_semantics=("arbitrary",)),
        name="ada",
    )(c, w_ada, b_ada.reshape(1, n_out))


def _proj_kernel(x_ref, mod_ref, g_ref, w_ref, prev0_ref, mu_ref,
                 wwa_ref, w0_ref, a0_ref, gup_ref, kk_ref, ka_ref, rk_ref, ones_ref,
                 wkv_ref, lw_ref, post_ref, q_ref, k_ref, v_ref, gt_ref, ptail_ref, ktail_ref, vtail_ref,
                 h_scr, prev_scr, *, n_tail, tiles_per_seq):
    nb, rows, d = x_ref.shape
    n = nb * rows
    step = pl.program_id(0)
    t_out = jnp.maximum(step - 1, 0) % tiles_per_seq

    @pl.when(step == 0)
    def _():
        h_scr[...] = jnp.zeros(h_scr.shape, BF16)

    @pl.when(t_out == 0)
    def _():
        prev_scr[...] = prev0_ref[...]

    mod = mod_ref[...]
    sh = mod[:, :, 0:d]
    sc = mod[:, :, d:2 * d]
    h_next = ((_rms(x_ref[...]) * g_ref[...]) * (1.0 + sc) + sh).reshape(nb * rows, d).astype(BF16)

    def col(lo, hi):
        return _dot(h_scr[...], w_ref[:, lo:hi])

    first_row = lax.broadcasted_iota(jnp.int32, (1, rows, 1), 1) == 0

    def shift_mix(lo, hi, p2):
        cs = slice(lo, hi)
        p = p2.reshape(nb, rows, hi - lo)
        shifted = pltpu.roll(p2, 1, axis=0).reshape(nb, rows, hi - lo)
        p_prev = jnp.where(first_row, prev_scr[:, 0:1, cs], shifted)
        prev_scr[:, 0:1, cs] = p[:, rows - 1:rows, :]
        ptail_ref[:, :, cs] = p[:, rows - 8:, :]
        return (p + (p_prev - p) * mu_ref[:, :, cs]).reshape(n, hi - lo)

    def put(ref, j, val):
        ref[:, :, j * D_R:(j + 1) * D_R] = val.reshape(nb, rows, D_R).astype(ref.dtype)

    mixed = {}

    def finish_r(p2):
        mixed["r"] = shift_mix(0, D_R, p2)
        put(wkv_ref, 0, mixed["r"])

    def finish_k(p2):
        mixed["k"] = shift_mix(D_R, 2 * D_R, p2)

    def finish_rest(p2):
        xs = shift_mix(2 * D_R, R_PROJ, p2)
        r, k, v = mixed["r"], mixed["k"], xs[:, 0:D_R]
        x_wa = xs[:, D_R:D_R + LANES]
        x_g = xs[:, D_R + LANES:D_R + 2 * LANES]
        lane = lax.broadcasted_iota(jnp.int32, (n, LANES), 1)
        wa = _dot(jnp.where(lane < LORA_W, jnp.tanh(x_wa), x_wa).astype(BF16), wwa_ref[...])
        lw = -math.exp(-0.5) * _sigmoid(wa[:, 0:D_R] + w0_ref[...])
        a = _sigmoid(wa[:, D_R:2 * D_R] + a0_ref[...])
        g = _dot(_sigmoid(x_g).astype(BF16), gup_ref[...])
        head_ones = ones_ref[...]
        kk = k * kk_ref[...]
        kk = kk * lax.rsqrt(jnp.maximum(_head_sums(kk * kk, head_ones), 1e-24))
        k2 = k * (1.0 + (a - 1.0) * ka_ref[...])
        bonus = _head_sums(r * k2 * rk_ref[...], head_ones) * v
        put(wkv_ref, 1, k2)
        put(wkv_ref, 2, v)
        put(wkv_ref, 3, kk)
        put(wkv_ref, 4, a * kk)
        lw_ref[...] = lw.reshape(nb, rows, D_R)
        put(post_ref, 0, g)
        put(post_ref, 1, bonus)

    def plain(ref, cs, fn=None, tail=None):
        def finish(t2):
            t = t2.reshape(nb, rows, t2.shape[1])
            ref[:, :, cs] = (t if fn is None else fn(t)).astype(BF16)
            if tail is not None:
                tail[...] = t[:, rows - n_tail:, :]
        return finish

    g0 = R_PROJ + 3 * D_A
    full = slice(None)
    jobs = [
        (0, D_R, finish_r),
        (D_R, 2 * D_R, finish_k),
        (2 * D_R, R_PROJ, finish_rest),
        (g0, g0 + d, plain(gt_ref, slice(0, d), fn=_sigmoid)),
        (g0 + d, g0 + 2 * d, plain(gt_ref, slice(d, 2 * d), fn=_sigmoid)),
        (R_PROJ + D_A, R_PROJ + 2 * D_A, plain(k_ref, full, tail=ktail_ref)),
        (R_PROJ + 2 * D_A, g0, plain(v_ref, full, tail=vtail_ref)),
        (R_PROJ, R_PROJ + D_A, plain(q_ref, full, fn=lambda t: t * (ATT_SCALE * LOG2E))),
    ]
    pending = None
    for lo, hi, finish in jobs:
        res = col(lo, hi)
        if pending is not None:
            pending[0](pending[1])
        pending = (finish, res)
    pending[0](pending[1])

    h_scr[...] = h_next


def _proj(x, mod, norm_g, w_in, prev0, mu, wp, *, nb, rows, n_tail):
    b, t, d = x.shape
    vec = _const_spec((1, D_R))
    tiles_per_seq = t // rows
    n_tiles = (b // nb) * tiles_per_seq
    t_in = lambda s: jnp.minimum(s, n_tiles - 1)
    t_out = lambda s: jnp.maximum(s - 1, 0)
    tile_in = lambda n: pl.BlockSpec(
        (nb, rows, n), lambda s: (t_in(s) // tiles_per_seq, t_in(s) % tiles_per_seq, 0))
    tile = lambda n: pl.BlockSpec(
        (nb, rows, n), lambda s: (t_out(s) // tiles_per_seq, t_out(s) % tiles_per_seq, 0))
    tail = lambda r, n: pl.BlockSpec((nb, r, n), lambda s: (t_out(s) // tiles_per_seq, 0, 0))
    out_shape = (
        jax.ShapeDtypeStruct((b, t, 5 * D_R), BF16),
        jax.ShapeDtypeStruct((b, t, D_R), F32),
        jax.ShapeDtypeStruct((b, t, 2 * D_R), BF16),
        jax.ShapeDtypeStruct((b, t, D_A), BF16),
        jax.ShapeDtypeStruct((b, t, D_A), BF16),
        jax.ShapeDtypeStruct((b, t, D_A), BF16),
        jax.ShapeDtypeStruct((b, t, 2 * d), BF16),
        jax.ShapeDtypeStruct((b, 8, R_PROJ), F32),
        jax.ShapeDtypeStruct((b, n_tail, D_A), F32),
        jax.ShapeDtypeStruct((b, n_tail, D_A), F32),
    )
    return pl.pallas_call(
        functools.partial(_proj_kernel, n_tail=n_tail, tiles_per_seq=tiles_per_seq),
        grid=(n_tiles + 1,),
        in_specs=[tile_in(d),
                  pl.BlockSpec((nb, 1, 6 * d), lambda s: (t_in(s) // tiles_per_seq, 0, 0)),
                  _const_spec((1, 1, d)),
                  _const_spec((d, IN_PROJ)),
                  tail(8, R_PROJ),
                  _const_spec((1, 1, R_PROJ)),
                  _const_spec((LANES, 2 * D_R)), vec, vec, _const_spec((LORA_G, D_R)), vec, vec, vec,
                  _const_spec((HEAD_SUM_BLOCK, HEAD_SUM_BLOCK))],
        out_specs=(tile(5 * D_R), tile(D_R), tile(2 * D_R), tile(D_A), tile(D_A), tile(D_A), tile(2 * d),
                   tail(8, R_PROJ), tail(n_tail, D_A), tail(n_tail, D_A)),
        out_shape=out_shape,
        scratch_shapes=[pltpu.VMEM((nb * rows, d), BF16),
                        pltpu.VMEM((nb, 8, R_PROJ), F32)],
        compiler_params=pltpu.CompilerParams(
            dimension_semantics=("arbitrary",), vmem_limit_bytes=VMEM_LIMIT),
        name="proj",
    )(x, mod, norm_g.reshape(1, 1, d), w_in, prev0, mu.reshape(1, 1, R_PROJ), *wp)


def _pair_blocks(x, m0):
    xb = x.astype(BF16)
    z = jnp.zeros_like(xb)
    return jnp.concatenate([jnp.where(m0, xb, z), jnp.where(m0, z, xb)], axis=0)


def _wkv_chunk_local(units, masks):
    m0, strict, incl, blockdiag = masks
    c = CHUNK
    row_c = lax.broadcasted_iota(jnp.int32, (c, LANES), 0)
    col_c = lax.broadcasted_iota(jnp.int32, (c, LANES), 1) & (HEAD_DIM - 1)
    eye = jnp.where(incl, 1.0, 0.0) - jnp.where(strict, 1.0, 0.0)
    zero = jnp.zeros((c, LANES), F32)
    zero2 = jnp.zeros((LANES, LANES), F32)
    pre = []
    for r, k, v, kk, b, lw, cum in units:
        r_t = r * jnp.exp(cum)
        kk_t = kk * jnp.exp(cum - lw)
        e_inv = jnp.exp(-cum)
        cum_last = cum[c - 1:c, :]
        dec = jnp.exp(cum_last - cum)
        pre.append(dict(
            r_t=r_t, kk_t=kk_t, v=v, lhs=jnp.concatenate([kk_t, r_t], axis=0).astype(BF16),
            k_h=_pair_blocks(k * e_inv, m0), b_h=_pair_blocks(b * e_inv, m0),
            v_blk=_pair_blocks(v, m0), k_p=k * dec, b_p=b * dec, p_last=jnp.exp(cum_last)))
    g12 = [_dot_nt(p["lhs"], jnp.concatenate([p["k_h"], p["b_h"]], axis=0)) for p in pre]
    m_k = [jnp.where(strict, g[:c, :LANES], zero).astype(BF16) for g in g12]
    a_rk = [jnp.where(incl, g[c:, :LANES], zero) for g in g12]
    n_low = [jnp.where(strict, g[:c, LANES:], zero) for g in g12]
    a_rb = [jnp.where(incl, g[c:, LANES:], zero) for g in g12]
    m_kv = [_dot(m, p["v_blk"]) for m, p in zip(m_k, pre)]

    def joins(level):
        return jnp.where(((row_c >> level) ^ (col_c >> level)) == 1, 1.0, 0.0)

    t_inv = [eye - n * joins(0) for n in n_low]
    for level in range(1, int(math.log2(c))):
        c_blk = [_pair_blocks(n * joins(level), m0) for n in n_low]
        t_c = [_dot(t.astype(BF16), cb) for t, cb in zip(t_inv, c_blk)]
        t_inv = [t - _dot(tc.astype(BF16), _pair_blocks(t, m0)) for t, tc in zip(t_inv, t_c)]
    y = [_dot(t.astype(BF16),
              jnp.concatenate([_pair_blocks(mv, m0), _pair_blocks(p["kk_t"], m0)], axis=1))
         for t, mv, p in zip(t_inv, m_kv, pre)]
    u_loc = [yi[:, :LANES] for yi in y]
    q = [yi[:, LANES:] for yi in y]

    zero_blk = jnp.zeros((2 * c, LANES), BF16)
    ow = [_dot(jnp.concatenate([ak, ab], axis=1).astype(BF16),
               jnp.concatenate([jnp.concatenate([p["v_blk"], zero_blk], axis=1),
                                jnp.concatenate([_pair_blocks(-u, m0), _pair_blocks(qi, m0)], axis=1)],
                               axis=0))
          for ak, ab, u, qi, p in zip(a_rk, a_rb, u_loc, q, pre)]
    o_loc = [x[:, :LANES] for x in ow]
    r_eff = [p["r_t"] - x[:, LANES:] for p, x in zip(pre, ow)]
    d_s = [jnp.where(blockdiag,
                     _dot_tn(jnp.concatenate([p["v"], u], axis=0).astype(BF16),
                             jnp.concatenate([p["k_p"], -p["b_p"]], axis=0).astype(BF16)), zero2)
           for p, u in zip(pre, u_loc)]
    qtb = [jnp.where(blockdiag, _dot_tn(qi.astype(BF16), p["b_p"].astype(BF16)), zero2)
           for p, qi in zip(pre, q)]
    return [(re, ol, qb, ds, p["p_last"]) for re, ol, qb, ds, p in zip(r_eff, o_loc, qtb, d_s, pre)]


def _wkv_kernel(x_ref, lw_ref, s0_ref, o_ref, sout_ref,
                s_scr, oloc_scr, reff_scr, qtb_scr, ds_scr, pc_scr):
    n_seq, rows, _ = x_ref.shape
    n_chunks = rows // CHUNK
    t_idx = pl.program_id(1)

    @pl.when(t_idx == 0)
    def _():
        s_scr[...] = s0_ref[...]

    ci = lax.broadcasted_iota(jnp.int32, (CHUNK, CHUNK), 0)
    cj = lax.broadcasted_iota(jnp.int32, (CHUNK, CHUNK), 1)
    tri = jnp.where(cj <= ci, 1.0, 0.0).astype(BF16)
    lane_c = lax.broadcasted_iota(jnp.int32, (CHUNK, LANES), 1)
    row_c = lax.broadcasted_iota(jnp.int32, (CHUNK, LANES), 0)
    m0 = lane_c < HEAD_DIM
    jj = jnp.where(m0, lane_c, lane_c - HEAD_DIM)
    br = lax.broadcasted_iota(jnp.int32, (LANES, LANES), 0) < HEAD_DIM
    bc = lax.broadcasted_iota(jnp.int32, (LANES, LANES), 1) < HEAD_DIM
    masks = (m0, jj < row_c, jj <= row_c, br == bc)

    group = math.gcd(n_chunks, max(1, WKV_GROUP_UNITS // (n_seq * N_PAIRS)))

    def unit_index(si, chunk, pr):
        return (si * n_chunks + chunk) * N_PAIRS + pr

    def local_body(gi, carry):
        units = []
        for si in range(n_seq):
            for cc in range(group):
                sl = pl.ds(pl.multiple_of((gi * group + cc) * CHUNK, CHUNK), CHUNK)
                lw_c = lw_ref[si, sl, :]
                cum = _split_dot_lhs(tri, lw_c)
                for pr in range(N_PAIRS):
                    ls = slice(pr * LANES, (pr + 1) * LANES)
                    cols = [x_ref[si, sl, j * D_R + pr * LANES:j * D_R + (pr + 1) * LANES].astype(F32)
                            for j in range(5)]
                    units.append((*cols, lw_c[:, ls], cum[:, ls]))
        res = iter(_wkv_chunk_local(units, masks))
        for si in range(n_seq):
            for cc in range(group):
                chunk = gi * group + cc
                sl = pl.ds(pl.multiple_of(chunk * CHUNK, CHUNK), CHUNK)
                for pr in range(N_PAIRS):
                    r_eff, o_loc, qtb, d_s, p_last = next(res)
                    u = unit_index(si, chunk, pr)
                    reff_scr[u] = r_eff.astype(BF16)
                    oloc_scr[si, sl, pr * LANES:(pr + 1) * LANES] = o_loc
                    qtb_scr[u] = qtb.astype(BF16)
                    ds_scr[u] = d_s
                    pc_scr[u] = jnp.broadcast_to(p_last, (8, LANES))
        return carry

    lax.fori_loop(0, n_chunks // group, local_body, 0)

    chains = [(si, pr) for si in range(n_seq) for pr in range(N_PAIRS)]

    def state_body(chunk, carry):
        sl = pl.ds(pl.multiple_of(chunk * CHUNK, CHUNK), CHUNK)
        s_old = [s_scr[si, pr] for si, pr in chains]
        s_bf = [s.astype(BF16) for s in s_old]
        us = [unit_index(si, chunk, pr) for si, pr in chains]
        o_add = [_dot_nt(reff_scr[u], sb) for u, sb in zip(us, s_bf)]
        s_mix = [_dot(sb, qtb_scr[u]) for u, sb in zip(us, s_bf)]
        for i, (si, pr) in enumerate(chains):
            ls = slice(pr * LANES, (pr + 1) * LANES)
            o_ref[si, sl, ls] = (oloc_scr[si, sl, ls] + o_add[i]).astype(BF16)
            s_scr[si, pr] = s_old[i] * pc_scr[us[i]][0:1, :] + ds_scr[us[i]] - s_mix[i]
        return carry

    lax.fori_loop(0, n_chunks, state_body, 0)

    @pl.when(t_idx == pl.num_programs(1) - 1)
    def _():
        sout_ref[...] = s_scr[...]


def _split_dot_lhs(tri, x):
    hi = x.astype(BF16)
    r1 = x - hi.astype(F32)
    mid = r1.astype(BF16)
    lo = (r1 - mid.astype(F32)).astype(BF16)
    return _dot(tri, hi) + _dot(tri, mid) + _dot(tri, lo)


def _wkv(x, lw, s0_pairs, *, rows, n_seq):
    b, t, _ = x.shape
    n_units = n_seq * rows // CHUNK * N_PAIRS
    tile = lambda n: pl.BlockSpec((n_seq, rows, n), lambda i, j: (i, j, 0))
    state = pl.BlockSpec((n_seq, N_PAIRS, LANES, LANES), lambda i, j: (i, 0, 0, 0))
    return pl.pallas_call(
        _wkv_kernel,
        grid=(b // n_seq, t // rows),
        in_specs=[tile(5 * D_R), tile(D_R), state],
        out_specs=(tile(D_R), state),
        out_shape=(jax.ShapeDtypeStruct((b, t, D_R), BF16),
                   jax.ShapeDtypeStruct((b, N_PAIRS, LANES, LANES), F32)),
        scratch_shapes=[pltpu.VMEM((n_seq, N_PAIRS, LANES, LANES), F32),
                        pltpu.VMEM((n_seq, rows, D_R), F32),
                        pltpu.VMEM((n_units, CHUNK, LANES), BF16),
                        pltpu.VMEM((n_units, LANES, LANES), BF16),
                        pltpu.VMEM((n_units, LANES, LANES), F32),
                        pltpu.VMEM((n_units, 8, LANES), F32)],
        compiler_params=pltpu.CompilerParams(
            dimension_semantics=("arbitrary", "arbitrary"), vmem_limit_bytes=VMEM_LIMIT),
        name="wkv",
    )(x, lw, s0_pairs)


def _state_to_pairs(s):
    b = s.shape[0]
    s = s.reshape(b, N_PAIRS, 2, HEAD_DIM, HEAD_DIM)
    z = jnp.zeros_like(s[:, :, 0])
    top = jnp.concatenate([s[:, :, 0], z], axis=-1)
    bot = jnp.concatenate([z, s[:, :, 1]], axis=-1)
    return jnp.concatenate([top, bot], axis=-2)


def _pairs_to_state(sp):
    b = sp.shape[0]
    h0 = sp[:, :, :HEAD_DIM, :HEAD_DIM]
    h1 = sp[:, :, HEAD_DIM:, HEAD_DIM:]
    return jnp.stack([h0, h1], axis=2).reshape(b, N_HEADS, HEAD_DIM, HEAD_DIM)


def _softmax_pv(units, m0):
    mx = []
    for parts in units:
        m = None
        for s, _ in parts:
            mi = jnp.max(s, axis=-1, keepdims=True)
            m = mi if m is None else jnp.maximum(m, mi)
        mx.append(m)
    es = [[jnp.exp2(s - m) for s, _ in parts] for parts, m in zip(units, mx)]
    ls = [functools.reduce(lambda a, b: a + b, [jnp.sum(e, axis=-1, keepdims=True) for e in ep])
          for ep in es]
    pvs = [functools.reduce(lambda a, b: a + b,
                            [_dot(e.astype(BF16), vals) for e, (_, vals) in zip(ep, parts)])
           for ep, parts in zip(es, units)]
    out = []
    for pv, l in zip(pvs, ls):
        pv = pv / l
        c = pv.shape[0] // 2
        out.append(jnp.where(m0, pv[:c], pv[c:]))
    return out


def _pair_queries(q, m0):
    z = jnp.zeros_like(q)
    return jnp.concatenate([jnp.where(m0, q, z), jnp.where(m0, z, q)], axis=0)


def _attn_prompt_kernel(q_ref, kc_ref, kp_ref, vc_ref, vp_ref, bias_ref, y_ref, k_scr, v_scr):
    rows = q_ref.shape[1]
    n_chunks = rows // CHUNK
    t_idx = pl.program_id(1)
    k_scr[0:WINDOW, :] = kp_ref[0]
    k_scr[WINDOW:WINDOW + rows, :] = kc_ref[0]
    v_scr[0:WINDOW, :] = vp_ref[0]
    v_scr[WINDOW:WINDOW + rows, :] = vc_ref[0]
    m0 = lax.broadcasted_iota(jnp.int32, (CHUNK, LANES), 1) < HEAD_DIM
    key_i = lax.broadcasted_iota(jnp.int32, (1, BAND), 1)
    ri = lax.broadcasted_iota(jnp.int32, (2 * CHUNK, 2 * CHUNK), 0)
    rj = lax.broadcasted_iota(jnp.int32, (2 * CHUNK, 2 * CHUNK), 1)
    eye = jnp.where(ri == rj, 1.0, 0.0).astype(BF16)

    group = math.gcd(n_chunks, ATTN_GROUP_CHUNKS)

    def group_body(gi, carry, *, masked):
        units = []
        for cc in range(group):
            r0 = pl.multiple_of((gi * group + cc) * CHUNK, CHUNK)
            band = pl.ds(r0, BAND)
            for pr in range(N_PAIRS):
                ls = slice(pr * LANES, (pr + 1) * LANES)
                lhs = jnp.concatenate([_pair_queries(q_ref[0, pl.ds(r0, CHUNK), ls], m0), eye], axis=1)
                rhs = jnp.concatenate([k_scr[band, ls], bias_ref[pr]], axis=1)
                s = _dot_nt(lhs, rhs)
                if masked:
                    s = s + jnp.where(key_i + (r0 - WINDOW) >= 0, 0.0, NEG_BIG)
                units.append([(s, v_scr[band, ls])])
        outs = _softmax_pv(units, m0)
        for cc in range(group):
            r0 = pl.multiple_of((gi * group + cc) * CHUNK, CHUNK)
            for pr in range(N_PAIRS):
                y_ref[0, pl.ds(r0, CHUNK), pr * LANES:(pr + 1) * LANES] = (
                    outs[cc * N_PAIRS + pr].astype(BF16))
        return carry

    @pl.when(t_idx == 0)
    def _():
        lax.fori_loop(0, n_chunks // group, functools.partial(group_body, masked=True), 0)

    @pl.when(t_idx > 0)
    def _():
        lax.fori_loop(0, n_chunks // group, functools.partial(group_body, masked=False), 0)


def _attn_prompt(q, k, v, bias_pairs, *, rows):
    b, t, _ = q.shape
    assert rows == WINDOW
    cur = pl.BlockSpec((1, rows, D_A), lambda i, j: (i, j, 0))
    prev = pl.BlockSpec((1, rows, D_A), lambda i, j: (i, jnp.maximum(j - 1, 0), 0))
    return pl.pallas_call(
        _attn_prompt_kernel,
        grid=(b, t // rows),
        in_specs=[cur, cur, prev, cur, prev, _const_spec((N_PAIRS, BAND, LANES))],
        out_specs=cur,
        out_shape=jax.ShapeDtypeStruct((b, t, D_A), BF16),
        scratch_shapes=[pltpu.VMEM((WINDOW + rows, D_A), BF16),
                        pltpu.VMEM((WINDOW + rows, D_A), BF16)],
        compiler_params=pltpu.CompilerParams(
            dimension_semantics=("arbitrary", "arbitrary"), vmem_limit_bytes=VMEM_LIMIT),
        name="attn_prompt",
    )(q, k, k, v, v, bias_pairs)


def _attn_sample_kernel(q_ref, k_ref, v_ref, ck_ref, cv_ref, bias_c_ref, bias_n_ref, y_ref):
    rows = q_ref.shape[1]
    m0 = lax.broadcasted_iota(jnp.int32, (rows, LANES), 1) < HEAD_DIM
    units = []
    for pr in range(N_PAIRS):
        ls = slice(pr * LANES, (pr + 1) * LANES)
        lhs = _pair_queries(q_ref[0, :, ls], m0)
        s_c = _dot_nt(lhs, ck_ref[0, :, ls].astype(BF16)) + bias_c_ref[pr]
        s_n = _dot_nt(lhs, k_ref[0, :, ls]) + bias_n_ref[pr]
        units.append([(s_c, cv_ref[0, :, ls].astype(BF16)), (s_n, v_ref[0, :, ls])])
    for pr, o in enumerate(_softmax_pv(units, m0)):
        y_ref[0, :, pr * LANES:(pr + 1) * LANES] = o.astype(BF16)


def _attn_sample(q, k, v, cache_k, cache_v, bias_c, bias_n):
    b, rows, _ = q.shape
    n_cache = cache_k.shape[1]
    cur = pl.BlockSpec((1, rows, D_A), lambda i: (i, 0, 0))
    cache = pl.BlockSpec((1, n_cache, D_A), lambda i: (i, 0, 0))
    return pl.pallas_call(
        _attn_sample_kernel,
        grid=(b,),
        in_specs=[cur, cur, cur, cache, cache,
                  _const_spec((N_PAIRS, 2 * rows, n_cache)), _const_spec((N_PAIRS, 2 * rows, rows))],
        out_specs=cur,
        out_shape=jax.ShapeDtypeStruct((b, rows, D_A), BF16),
        compiler_params=pltpu.CompilerParams(dimension_semantics=("arbitrary",)),
        name="attn_sample",
    )(q, k, v, cache_k, cache_v, bias_c, bias_n)


def _rel_bias(table, n_q, n_k, offset):
    h = table.shape[0]
    period = n_q + n_k
    e = jnp.arange(period)
    e = jnp.where(e < n_k, e, e - period)
    u = table[:, jnp.clip(offset - e, -REL_CLIP, REL_CLIP) + REL_CLIP].astype(F32) * LOG2E
    toeplitz = jnp.tile(u, (1, n_q))[:, :n_q * (period - 1)].reshape(h, n_q, period - 1)
    return toeplitz[:, :, :n_k]


def _pair_rows(bias):
    h, nq, nk = bias.shape
    return bias.reshape(h // 2, 2 * nq, nk)


def _out_kernel(x_ref, o_ref, post_ref, ya_ref, gt_ref, mod_ref, lng_ref, lnb_ref, ones_ref,
                wbr_ref, wba_ref, wo_ref, ng_ref, w1_ref, w2_ref, fg_ref, y_ref):
    nb, rows, d = x_ref.shape
    n = nb * rows
    mod = mod_ref[...]
    g1 = mod[:, :, 2 * d:3 * d]
    sh2 = mod[:, :, 3 * d:4 * d]
    sc2 = mod[:, :, 4 * d:5 * d]
    g2 = mod[:, :, 5 * d:6 * d]

    gates = gt_ref[...].reshape(n, 2 * d).astype(F32)
    branch_a = gates[:, d:2 * d] * _dot(ya_ref[...].reshape(n, D_A), wba_ref[...])

    head_ones = ones_ref[...]
    o = o_ref[...].reshape(n, D_R)
    post = post_ref[...].reshape(n, 2 * D_R).astype(F32)
    inv_n = 1.0 / HEAD_DIM
    mean = _head_sums(o, head_ones) * inv_n
    cen = o.astype(F32) - mean
    var = _head_sums(cen * cen, head_ones) * inv_n
    o_n = cen * lax.rsqrt(var + GN_EPS) * lng_ref[...] + lnb_ref[...]
    y_r = ((o_n + post[:, D_R:2 * D_R]) * post[:, 0:D_R]).astype(BF16)

    merged = gates[:, 0:d] * _dot(y_r, wbr_ref[...]) + branch_a
    mix = _dot(merged.astype(BF16), wo_ref[...]).reshape(nb, rows, d)
    x1 = x_ref[...] + g1 * mix

    h2 = ((_rms(x1) * ng_ref[...]) * (1.0 + sc2) + sh2).reshape(n, d).astype(BF16)
    ff_blk = 1024
    n_blk = D_FF // ff_blk
    acc = None
    mid = _dot(h2, w1_ref[:, 0:ff_blk])
    for j in range(n_blk):
        nxt = _dot(h2, w1_ref[:, (j + 1) * ff_blk:(j + 2) * ff_blk]) if j + 1 < n_blk else None
        act = jnp.square(jnp.maximum(mid, 0.0)).astype(BF16)
        part = _dot(act, w2_ref[j * ff_blk:(j + 1) * ff_blk, :])
        acc = part if acc is None else acc + part
        mid = nxt
    x2 = x1 + g2 * acc.reshape(nb, rows, d)
    y_ref[...] = _rms(x2) * fg_ref[...]


def _out(x, o, post, y_a, gates, mod, lnx_g, lnx_b, head_ones,
         w_br_r, w_br_a, w_out, norm_g, w_ff1, w_ff2, final_g, *, nb, rows):
    b, t, d = x.shape
    tile = lambda n: pl.BlockSpec((nb, rows, n), lambda i, j: (i, j, 0))
    return pl.pallas_call(
        _out_kernel,
        grid=(b // nb, t // rows),
        in_specs=[tile(d), tile(D_R), tile(2 * D_R), tile(D_A), tile(2 * d),
                  pl.BlockSpec((nb, 1, 6 * d), lambda i, j: (i, 0, 0)),
                  _const_spec((1, D_R)), _const_spec((1, D_R)), _const_spec((HEAD_SUM_BLOCK, HEAD_SUM_BLOCK)),
                  _const_spec((D_R, d)), _const_spec((D_A, d)), _const_spec((d, d)),
                  _const_spec((1, 1, d)),
                  _const_spec((d, D_FF)), _const_spec((D_FF, d)),
                  _const_spec((1, 1, d))],
        out_specs=tile(d),
        out_shape=jax.ShapeDtypeStruct((b, t, d), F32),
        compiler_params=pltpu.CompilerParams(
            dimension_semantics=("arbitrary", "arbitrary"), vmem_limit_bytes=VMEM_LIMIT),
        name="out",
    )(x, o, post, y_a, gates, mod, lnx_g, lnx_b, head_ones,
      w_br_r, w_br_a, w_out, norm_g.reshape(1, 1, d), w_ff1, w_ff2, final_g.reshape(1, 1, d))


def _group(x, mod, prev0, s0, attend, W, *, nb, rows, n_tail):
    b, t, d = x.shape
    wkv_in, lw, post, q, k, v, gates, p_tail, k_tail, v_tail = _proj(
        x, mod, W["norm_mix_g"], W["w_in"], prev0, W["mu_shift"], W["rwkv_in"],
        nb=nb, rows=rows, n_tail=n_tail)
    t_pad = -(-t // CHUNK) * CHUNK
    pad = lambda a: jnp.pad(a, ((0, 0), (0, t_pad - t), (0, 0)))
    o, s_pairs = _wkv(pad(wkv_in), pad(lw), _state_to_pairs(s0),
                      rows=min(t_pad, WKV_ROWS), n_seq=math.gcd(b, WKV_SEQS))
    y_a = attend(q, k, v)
    y = _out(x, o[:, :t], post, y_a, gates, mod, W["lnx_g"], W["lnx_b"], W["rwkv_in"][-1],
             W["w_br_r"], W["w_br_a"], W["w_out"], W["norm_mlp_g"],
             W["w_ff1"], W["w_ff2"], W["final_norm_g"], nb=nb, rows=rows)
    return (y, _pairs_to_state(s_pairs)[None], p_tail[None, :, 7, :],
            k_tail.reshape(1, b, n_tail, N_HEADS, HEAD_DIM),
            v_tail.reshape(1, b, n_tail, N_HEADS, HEAD_DIM))


def kernel(x_prompt, x_sample, c_prompt, c_sample, state_rwkv_wkv, state_rwkv_shift, cache_att_k, cache_att_v, w_ada, b_ada, norm_mix_g, w_in, mu_shift, w_lora_up, w0, a_lora_up, a0, g_lora_up, k_k, k_a, r_k, lnx_g, lnx_b, rel_table, w_br_r, w_br_a, w_out, norm_mlp_g, w_ff1, w_ff2, final_norm_g):
    bp, tp, d = x_prompt.shape
    bs, ts, _ = x_sample.shape
    l = 0
    row = lambda a: a.reshape(1, -1)
    zl = jnp.zeros((LORA_W, D_R), F32)
    w_wa = jnp.concatenate([jnp.concatenate([w_lora_up[l], zl], axis=1),
                            jnp.concatenate([zl, a_lora_up[l]], axis=1)], axis=0).astype(BF16)
    head_id = jnp.arange(HEAD_SUM_BLOCK) // HEAD_DIM
    head_ones = (head_id[:, None] == head_id[None, :]).astype(BF16)
    W = dict(
        norm_mix_g=norm_mix_g[l], w_in=w_in[l].astype(BF16), mu_shift=mu_shift[l],
        rwkv_in=(w_wa, row(w0[l]), row(a0[l]), g_lora_up[l].astype(BF16),
                 row(k_k[l]), row(k_a[l]), row(r_k[l]), head_ones),
        lnx_g=row(lnx_g[l]), lnx_b=row(lnx_b[l]),
        w_br_r=w_br_r[l].astype(BF16), w_br_a=w_br_a[l].astype(BF16), w_out=w_out[l].astype(BF16),
        norm_mlp_g=norm_mlp_g[l], w_ff1=w_ff1[l].astype(BF16), w_ff2=w_ff2[l].astype(BF16),
        final_norm_g=final_norm_g)

    mod = _ada(jnp.concatenate([c_prompt, c_sample], axis=0), w_ada[l], b_ada[l])
    mod_p = mod[:bp, None, :]
    mod_s = mod[bp:, None, :]

    bias_p = _rel_bias(rel_table[l], CHUNK, BAND, WINDOW)
    bias_t = bias_p.reshape(N_PAIRS, 2 * CHUNK, BAND).transpose(0, 2, 1).astype(BF16)
    attend_p = lambda q, k, v: _attn_prompt(q, k, v, bias_t, rows=WINDOW)
    n_keep = min(WINDOW, tp)
    y_p, p_wkv, p_shift, p_k, p_v = _group(
        x_prompt, mod_p, jnp.zeros((bp, 8, R_PROJ), F32),
        jnp.zeros((bp, N_HEADS, HEAD_DIM, HEAD_DIM), F32), attend_p, W,
        nb=1, rows=WINDOW, n_tail=n_keep)

    n_cache = cache_att_k.shape[2]
    bias_s = _pair_rows(_rel_bias(rel_table[l], ts, n_cache + ts, n_cache))
    ck = cache_att_k[l].reshape(bs, n_cache, D_A)
    cv = cache_att_v[l].reshape(bs, n_cache, D_A)
    attend_s = lambda q, k, v: _attn_sample(q, k, v, ck, cv, bias_s[:, :, :n_cache], bias_s[:, :, n_cache:])
    prev_s = jnp.broadcast_to(state_rwkv_shift[l][:, None, :], (bs, 8, R_PROJ))
    y_s, s_wkv, s_shift, s_k, s_v = _group(
        x_sample, mod_s, prev_s, state_rwkv_wkv[l], attend_s, W, nb=bs, rows=ts, n_tail=ts)

    return (y_p, y_s, p_wkv, p_shift, p_k, p_v, s_wkv, s_shift, s_k, s_v)
```

```python
import functools
import math

import jax
import jax.numpy as jnp
from jax import lax
from jax.experimental import pallas as pl
from jax.experimental.pallas import tpu as pltpu

D_MODEL = 1024
HEAD_DIM = 64
N_HEADS = 8
D_R = N_HEADS * HEAD_DIM
D_A = N_HEADS * HEAD_DIM
LORA_W = 64
LORA_A = 64
LORA_G = 128
R_PROJ = 3 * D_R + LORA_W + LORA_A + LORA_G
CHUNK = 64
BAND_CHUNKS = 8
WINDOW = BAND_CHUNKS * CHUNK
BAND = WINDOW + CHUNK
REL_CLIP = 128
D_FF = 4 * D_MODEL
IN_PROJ = R_PROJ + 3 * D_A + 2 * D_MODEL
NORM_EPS = 1e-6
GN_EPS = 64e-5
ATT_SCALE = HEAD_DIM ** -0.5
LOG2E = math.log2(math.e)
PAST_LEN = 1024

LANES = 128
N_PAIRS = N_HEADS // 2
NEG_BIG = -1e30
HEAD_SUM_BLOCK = 256
WKV_GROUP_UNITS = 16
WKV_SEQS = 4
WKV_ROWS = 256
ATTN_GROUP_CHUNKS = 8
VMEM_LIMIT = 56 * 1024 * 1024

F32 = jnp.float32
BF16 = jnp.bfloat16


def _dot(a, b):
    return lax.dot_general(a, b, (((1,), (0,)), ((), ())), preferred_element_type=F32)


def _dot_nt(a, b):
    return lax.dot_general(a, b, (((1,), (1,)), ((), ())), preferred_element_type=F32)


def _dot_tn(a, b):
    return lax.dot_general(a, b, (((0,), (0,)), ((), ())), preferred_element_type=F32)


def _head_sums(x, ones):
    xb = x.astype(BF16)
    w = ones.shape[0]
    return jnp.concatenate([_dot(xb[:, j:j + w], ones) for j in range(0, x.shape[1], w)], axis=1)


def _sigmoid(x):
    return 0.5 * jnp.tanh(0.5 * x) + 0.5


def _rms(x):
    return x * lax.rsqrt(jnp.mean(x * x, axis=-1, keepdims=True) + NORM_EPS)


def _const_spec(shape):
    nd = len(shape)
    return pl.BlockSpec(shape, lambda *_: (0,) * nd, pipeline_mode=pl.Buffered(1))


def _ada_kernel(c_ref, w_ref, b_ref, o_ref):
    c = c_ref[...]
    s = (c * jax.nn.sigmoid(c)).astype(BF16)
    o_ref[...] = _dot(s, w_ref[...].astype(BF16)) + b_ref[...]


def _ada(c, w_ada, b_ada):
    n, d = c.shape
    n_out = w_ada.shape[1]
    bn = 1024
    return pl.pallas_call(
        _ada_kernel,
        grid=(n_out // bn,),
        in_specs=[pl.BlockSpec((n, d), lambda j: (0, 0)),
                  pl.BlockSpec((d, bn), lambda j: (0, j)),
                  pl.BlockSpec((1, bn), lambda j: (0, j))],
        out_specs=pl.BlockSpec((n, bn), lambda j: (0, j)),
        out_shape=jax.ShapeDtypeStruct((n, n_out), F32),
        compiler_params=pltpu.CompilerParams(dimension_semantics=("arbitrary",)),
        name="ada",
    )(c, w_ada, b_ada.reshape(1, n_out))


def _proj_kernel(x_ref, mod_ref, g_ref, w_ref, prev0_ref, mu_ref,
                 wwa_ref, w0_ref, a0_ref, gup_ref, kk_ref, ka_ref, rk_ref, ones_ref,
                 wkv_ref, lw_ref, post_ref, q_ref, k_ref, v_ref, gt_ref, ptail_ref, ktail_ref, vtail_ref,
                 h_scr, prev_scr, *, n_tail, tiles_per_seq):
    nb, rows, d = x_ref.shape
    n = nb * rows
    step = pl.program_id(0)
    t_out = jnp.maximum(step - 1, 0) % tiles_per_seq

    @pl.when(step == 0)
    def _():
        h_scr[...] = jnp.zeros(h_scr.shape, BF16)

    @pl.when(t_out == 0)
    def _():
        prev_scr[...] = prev0_ref[...]

    mod = mod_ref[...]
    sh = mod[:, :, 0:d]
    sc = mod[:, :, d:2 * d]
    h_next = ((_rms(x_ref[...]) * g_ref[...]) * (1.0 + sc) + sh).reshape(nb * rows, d).astype(BF16)

    def col(lo, hi):
        return _dot(h_scr[...], w_ref[:, lo:hi])

    first_row = lax.broadcasted_iota(jnp.int32, (1, rows, 1), 1) == 0

    def shift_mix(lo, hi, p2):
        cs = slice(lo, hi)
        p = p2.reshape(nb, rows, hi - lo)
        shifted = pltpu.roll(p2, 1, axis=0).reshape(nb, rows, hi - lo)
        p_prev = jnp.where(first_row, prev_scr[:, 0:1, cs], shifted)
        prev_scr[:, 0:1, cs] = p[:, rows - 1:rows, :]
        ptail_ref[:, :, cs] = p[:, rows - 8:, :]
        return (p + (p_prev - p) * mu_ref[:, :, cs]).reshape(n, hi - lo)

    def put(ref, j, val):
        ref[:, :, j * D_R:(j + 1) * D_R] = val.reshape(nb, rows, D_R).astype(ref.dtype)

    mixed = {}

    def finish_r(p2):
        mixed["r"] = shift_mix(0, D_R, p2)
        put(wkv_ref, 0, mixed["r"])

    def finish_k(p2):
        mixed["k"] = shift_mix(D_R, 2 * D_R, p2)

    def finish_rest(p2):
        xs = shift_mix(2 * D_R, R_PROJ, p2)
        r, k, v = mixed["r"], mixed["k"], xs[:, 0:D_R]
        x_wa = xs[:, D_R:D_R + LANES]
        x_g = xs[:, D_R + LANES:D_R + 2 * LANES]
        lane = lax.broadcasted_iota(jnp.int32, (n, LANES), 1)
        wa = _dot(jnp.where(lane < LORA_W, jnp.tanh(x_wa), x_wa).astype(BF16), wwa_ref[...])
        lw = -math.exp(-0.5) * _sigmoid(wa[:, 0:D_R] + w0_ref[...])
        a = _sigmoid(wa[:, D_R:2 * D_R] + a0_ref[...])
        g = _dot(_sigmoid(x_g).astype(BF16), gup_ref[...])
        head_ones = ones_ref[...]
        kk = k * kk_ref[...]
        kk = kk * lax.rsqrt(jnp.maximum(_head_sums(kk * kk, head_ones), 1e-24))
        k2 = k * (1.0 + (a - 1.0) * ka_ref[...])
        bonus = _head_sums(r * k2 * rk_ref[...], head_ones) * v
        put(wkv_ref, 1, k2)
        put(wkv_ref, 2, v)
        put(wkv_ref, 3, kk)
        put(wkv_ref, 4, a * kk)
        lw_ref[...] = lw.reshape(nb, rows, D_R)
        put(post_ref, 0, g)
        put(post_ref, 1, bonus)

    def plain(ref, cs, fn=None, tail=None):
        def finish(t2):
            t = t2.reshape(nb, rows, t2.shape[1])
            ref[:, :, cs] = (t if fn is None else fn(t)).astype(BF16)
            if tail is not None:
                tail[...] = t[:, rows - n_tail:, :]
        return finish

    g0 = R_PROJ + 3 * D_A
    full = slice(None)
    jobs = [
        (0, D_R, finish_r),
        (D_R, 2 * D_R, finish_k),
        (2 * D_R, R_PROJ, finish_rest),
        (g0, g0 + d, plain(gt_ref, slice(0, d), fn=_sigmoid)),
        (g0 + d, g0 + 2 * d, plain(gt_ref, slice(d, 2 * d), fn=_sigmoid)),
        (R_PROJ + D_A, R_PROJ + 2 * D_A, plain(k_ref, full, tail=ktail_ref)),
        (R_PROJ + 2 * D_A, g0, plain(v_ref, full, tail=vtail_ref)),
        (R_PROJ, R_PROJ + D_A, plain(q_ref, full, fn=lambda t: t * (ATT_SCALE * LOG2E))),
    ]
    pending = None
    for lo, hi, finish in jobs:
        res = col(lo, hi)
        if pending is not None:
            pending[0](pending[1])
        pending = (finish, res)
    pending[0](pending[1])

    h_scr[...] = h_next


def _proj(x, mod, norm_g, w_in, prev0, mu, wp, *, nb, rows, n_tail):
    b, t, d = x.shape
    vec = _const_spec((1, D_R))
    tiles_per_seq = t // rows
    n_tiles = (b // nb) * tiles_per_seq
    t_in = lambda s: jnp.minimum(s, n_tiles - 1)
    t_out = lambda s: jnp.maximum(s - 1, 0)
    tile_in = lambda n: pl.BlockSpec(
        (nb, rows, n), lambda s: (t_in(s) // tiles_per_seq, t_in(s) % tiles_per_seq, 0))
    tile = lambda n: pl.BlockSpec(
        (nb, rows, n), lambda s: (t_out(s) // tiles_per_seq, t_out(s) % tiles_per_seq, 0))
    tail = lambda r, n: pl.BlockSpec((nb, r, n), lambda s: (t_out(s) // tiles_per_seq, 0, 0))
    out_shape = (
        jax.ShapeDtypeStruct((b, t, 5 * D_R), BF16),
        jax.ShapeDtypeStruct((b, t, D_R), F32),
        jax.ShapeDtypeStruct((b, t, 2 * D_R), BF16),
        jax.ShapeDtypeStruct((b, t, D_A), BF16),
        jax.ShapeDtypeStruct((b, t, D_A), BF16),
        jax.ShapeDtypeStruct((b, t, D_A), BF16),
        jax.ShapeDtypeStruct((b, t, 2 * d), BF16),
        jax.ShapeDtypeStruct((b, 8, R_PROJ), F32),
        jax.ShapeDtypeStruct((b, n_tail, D_A), F32),
        jax.ShapeDtypeStruct((b, n_tail, D_A), F32),
    )
    return pl.pallas_call(
        functools.partial(_proj_kernel, n_tail=n_tail, tiles_per_seq=tiles_per_seq),
        grid=(n_tiles + 1,),
        in_specs=[tile_in(d),
                  pl.BlockSpec((nb, 1, 6 * d), lambda s: (t_in(s) // tiles_per_seq, 0, 0)),
                  _const_spec((1, 1, d)),
                  _const_spec((d, IN_PROJ)),
                  tail(8, R_PROJ),
                  _const_spec((1, 1, R_PROJ)),
                  _const_spec((LANES, 2 * D_R)), vec, vec, _const_spec((LORA_G, D_R)), vec, vec, vec,
                  _const_spec((HEAD_SUM_BLOCK, HEAD_SUM_BLOCK))],
        out_specs=(tile(5 * D_R), tile(D_R), tile(2 * D_R), tile(D_A), tile(D_A), tile(D_A), tile(2 * d),
                   tail(8, R_PROJ), tail(n_tail, D_A), tail(n_tail, D_A)),
        out_shape=out_shape,
        scratch_shapes=[pltpu.VMEM((nb * rows, d), BF16),
                        pltpu.VMEM((nb, 8, R_PROJ), F32)],
        compiler_params=pltpu.CompilerParams(
            dimension_semantics=("arbitrary",), vmem_limit_bytes=VMEM_LIMIT),
        name="proj",
    )(x, mod, norm_g.reshape(1, 1, d), w_in, prev0, mu.reshape(1, 1, R_PROJ), *wp)


def _pair_blocks(x, m0):
    xb = x.astype(BF16)
    z = jnp.zeros_like(xb)
    return jnp.concatenate([jnp.where(m0, xb, z), jnp.where(m0, z, xb)], axis=0)


def _wkv_chunk_local(units, masks):
    m0, strict, incl, blockdiag = masks
    c = CHUNK
    row_c = lax.broadcasted_iota(jnp.int32, (c, LANES), 0)
    col_c = lax.broadcasted_iota(jnp.int32, (c, LANES), 1) & (HEAD_DIM - 1)
    eye = jnp.where(incl, 1.0, 0.0) - jnp.where(strict, 1.0, 0.0)
    zero = jnp.zeros((c, LANES), F32)
    zero2 = jnp.zeros((LANES, LANES), F32)
    pre = []
    for r, k, v, kk, b, lw, cum in units:
        r_t = r * jnp.exp(cum)
        kk_t = kk * jnp.exp(cum - lw)
        e_inv = jnp.exp(-cum)
        cum_last = cum[c - 1:c, :]
        dec = jnp.exp(cum_last - cum)
        pre.append(dict(
            r_t=r_t, kk_t=kk_t, v=v, lhs=jnp.concatenate([kk_t, r_t], axis=0).astype(BF16),
            k_h=_pair_blocks(k * e_inv, m0), b_h=_pair_blocks(b * e_inv, m0),
            v_blk=_pair_blocks(v, m0), k_p=k * dec, b_p=b * dec, p_last=jnp.exp(cum_last)))
    g12 = [_dot_nt(p["lhs"], jnp.concatenate([p["k_h"], p["b_h"]], axis=0)) for p in pre]
    m_k = [jnp.where(strict, g[:c, :LANES], zero).astype(BF16) for g in g12]
    a_rk = [jnp.where(incl, g[c:, :LANES], zero) for g in g12]
    n_low = [jnp.where(strict, g[:c, LANES:], zero) for g in g12]
    a_rb = [jnp.where(incl, g[c:, LANES:], zero) for g in g12]
    m_kv = [_dot(m, p["v_blk"]) for m, p in zip(m_k, pre)]

    def joins(level):
        return jnp.where(((row_c >> level) ^ (col_c >> level)) == 1, 1.0, 0.0)

    t_inv = [eye - n * joins(0) for n in n_low]
    for level in range(1, int(math.log2(c))):
        c_blk = [_pair_blocks(n * joins(level), m0) for n in n_low]
        t_c = [_dot(t.astype(BF16), cb) for t, cb in zip(t_inv, c_blk)]
        t_inv = [t - _dot(tc.astype(BF16), _pair_blocks(t, m0)) for t, tc in zip(t_inv, t_c)]
    y = [_dot(t.astype(BF16),
              jnp.concatenate([_pair_blocks(mv, m0), _pair_blocks(p["kk_t"], m0)], axis=1))
         for t, mv, p in zip(t_inv, m_kv, pre)]
    u_loc = [yi[:, :LANES] for yi in y]
    q = [yi[:, LANES:] for yi in y]

    zero_blk = jnp.zeros((2 * c, LANES), BF16)
    ow = [_dot(jnp.concatenate([ak, ab], axis=1).astype(BF16),
               jnp.concatenate([jnp.concatenate([p["v_blk"], zero_blk], axis=1),
                                jnp.concatenate([_pair_blocks(-u, m0), _pair_blocks(qi, m0)], axis=1)],
                               axis=0))
          for ak, ab, u, qi, p in zip(a_rk, a_rb, u_loc, q, pre)]
    o_loc = [x[:, :LANES] for x in ow]
    r_eff = [p["r_t"] - x[:, LANES:] for p, x in zip(pre, ow)]
    d_s = [jnp.where(blockdiag,
                     _dot_tn(jnp.concatenate([p["v"], u], axis=0).astype(BF16),
                             jnp.concatenate([p["k_p"], -p["b_p"]], axis=0).astype(BF16)), zero2)
           for p, u in zip(pre, u_loc)]
    qtb = [jnp.where(blockdiag, _dot_tn(qi.astype(BF16), p["b_p"].astype(BF16)), zero2)
           for p, qi in zip(pre, q)]
    return [(re, ol, qb, ds, p["p_last"]) for re, ol, qb, ds, p in zip(r_eff, o_loc, qtb, d_s, pre)]


def _wkv_kernel(x_ref, lw_ref, s0_ref, o_ref, sout_ref,
                s_scr, oloc_scr, reff_scr, qtb_scr, ds_scr, pc_scr):
    n_seq, rows, _ = x_ref.shape
    n_chunks = rows // CHUNK
    t_idx = pl.program_id(1)

    @pl.when(t_idx == 0)
    def _():
        s_scr[...] = s0_ref[...]

    ci = lax.broadcasted_iota(jnp.int32, (CHUNK, CHUNK), 0)
    cj = lax.broadcasted_iota(jnp.int32, (CHUNK, CHUNK), 1)
    tri = jnp.where(cj <= ci, 1.0, 0.0).astype(BF16)
    lane_c = lax.broadcasted_iota(jnp.int32, (CHUNK, LANES), 1)
    row_c = lax.broadcasted_iota(jnp.int32, (CHUNK, LANES), 0)
    m0 = lane_c < HEAD_DIM
    jj = jnp.where(m0, lane_c, lane_c - HEAD_DIM)
    br = lax.broadcasted_iota(jnp.int32, (LANES, LANES), 0) < HEAD_DIM
    bc = lax.broadcasted_iota(jnp.int32, (LANES, LANES), 1) < HEAD_DIM
    masks = (m0, jj < row_c, jj <= row_c, br == bc)

    group = math.gcd(n_chunks, max(1, WKV_GROUP_UNITS // (n_seq * N_PAIRS)))

    def unit_index(si, chunk, pr):
        return (si * n_chunks + chunk) * N_PAIRS + pr

    def local_body(gi, carry):
        units = []
        for si in range(n_seq):
            for cc in range(group):
                sl = pl.ds(pl.multiple_of((gi * group + cc) * CHUNK, CHUNK), CHUNK)
                lw_c = lw_ref[si, sl, :]
                cum = _split_dot_lhs(tri, lw_c)
                for pr in range(N_PAIRS):
                    ls = slice(pr * LANES, (pr + 1) * LANES)
                    cols = [x_ref[si, sl, j * D_R + pr * LANES:j * D_R + (pr + 1) * LANES].astype(F32)
                            for j in range(5)]
                    units.append((*cols, lw_c[:, ls], cum[:, ls]))
        res = iter(_wkv_chunk_local(units, masks))
        for si in range(n_seq):
            for cc in range(group):
                chunk = gi * group + cc
                sl = pl.ds(pl.multiple_of(chunk * CHUNK, CHUNK), CHUNK)
                for pr in range(N_PAIRS):
                    r_eff, o_loc, qtb, d_s, p_last = next(res)
                    u = unit_index(si, chunk, pr)
                    reff_scr[u] = r_eff.astype(BF16)
                    oloc_scr[si, sl, pr * LANES:(pr + 1) * LANES] = o_loc
                    qtb_scr[u] = qtb.astype(BF16)
                    ds_scr[u] = d_s
                    pc_scr[u] = jnp.broadcast_to(p_last, (8, LANES))
        return carry

    lax.fori_loop(0, n_chunks // group, local_body, 0)

    chains = [(si, pr) for si in range(n_seq) for pr in range(N_PAIRS)]

    def state_body(chunk, carry):
        sl = pl.ds(pl.multiple_of(chunk * CHUNK, CHUNK), CHUNK)
        s_old = [s_scr[si, pr] for si, pr in chains]
        s_bf = [s.astype(BF16) for s in s_old]
        us = [unit_index(si, chunk, pr) for si, pr in chains]
        o_add = [_dot_nt(reff_scr[u], sb) for u, sb in zip(us, s_bf)]
        s_mix = [_dot(sb, qtb_scr[u]) for u, sb in zip(us, s_bf)]
        for i, (si, pr) in enumerate(chains):
            ls = slice(pr * LANES, (pr + 1) * LANES)
            o_ref[si, sl, ls] = (oloc_scr[si, sl, ls] + o_add[i]).astype(BF16)
            s_scr[si, pr] = s_old[i] * pc_scr[us[i]][0:1, :] + ds_scr[us[i]] - s_mix[i]
        return carry

    lax.fori_loop(0, n_chunks, state_body, 0)

    @pl.when(t_idx == pl.num_programs(1) - 1)
    def _():
        sout_ref[...] = s_scr[...]


def _split_dot_lhs(tri, x):
    hi = x.astype(BF16)
    r1 = x - hi.astype(F32)
    mid = r1.astype(BF16)
    lo = (r1 - mid.astype(F32)).astype(BF16)
    return _dot(tri, hi) + _dot(tri, mid) + _dot(tri, lo)


def _wkv(x, lw, s0_pairs, *, rows, n_seq):
    b, t, _ = x.shape
    n_units = n_seq * rows // CHUNK * N_PAIRS
    tile = lambda n: pl.BlockSpec((n_seq, rows, n), lambda i, j: (i, j, 0))
    state = pl.BlockSpec((n_seq, N_PAIRS, LANES, LANES), lambda i, j: (i, 0, 0, 0))
    return pl.pallas_call(
        _wkv_kernel,
        grid=(b // n_seq, t // rows),
        in_specs=[tile(5 * D_R), tile(D_R), state],
        out_specs=(tile(D_R), state),
        out_shape=(jax.ShapeDtypeStruct((b, t, D_R), BF16),
                   jax.ShapeDtypeStruct((b, N_PAIRS, LANES, LANES), F32)),
        scratch_shapes=[pltpu.VMEM((n_seq, N_PAIRS, LANES, LANES), F32),
                        pltpu.VMEM((n_seq, rows, D_R), F32),
                        pltpu.VMEM((n_units, CHUNK, LANES), BF16),
                        pltpu.VMEM((n_units, LANES, LANES), BF16),
                        pltpu.VMEM((n_units, LANES, LANES), F32),
                        pltpu.VMEM((n_units, 8, LANES), F32)],
        compiler_params=pltpu.CompilerParams(
            dimension_semantics=("arbitrary", "arbitrary"), vmem_limit_bytes=VMEM_LIMIT),
        name="wkv",
    )(x, lw, s0_pairs)


def _state_to_pairs(s):
    b = s.shape[0]
    s = s.reshape(b, N_PAIRS, 2, HEAD_DIM, HEAD_DIM)
    z = jnp.zeros_like(s[:, :, 0])
    top = jnp.concatenate([s[:, :, 0], z], axis=-1)
    bot = jnp.concatenate([z, s[:, :, 1]], axis=-1)
    return jnp.concatenate([top, bot], axis=-2)


def _pairs_to_state(sp):
    b = sp.shape[0]
    h0 = sp[:, :, :HEAD_DIM, :HEAD_DIM]
    h1 = sp[:, :, HEAD_DIM:, HEAD_DIM:]
    return jnp.stack([h0, h1], axis=2).reshape(b, N_HEADS, HEAD_DIM, HEAD_DIM)


def _softmax_pv(units, m0):
    mx = []
    for parts in units:
        m = None
        for s, _ in parts:
            mi = jnp.max(s, axis=-1, keepdims=True)
            m = mi if m is None else jnp.maximum(m, mi)
        mx.append(m)
    es = [[jnp.exp2(s - m) for s, _ in parts] for parts, m in zip(units, mx)]
    ls = [functools.reduce(lambda a, b: a + b, [jnp.sum(e, axis=-1, keepdims=True) for e in ep])
          for ep in es]
    pvs = [functools.reduce(lambda a, b: a + b,
                            [_dot(e.astype(BF16), vals) for e, (_, vals) in zip(ep, parts)])
           for ep, parts in zip(es, units)]
    out = []
    for pv, l in zip(pvs, ls):
        pv = pv / l
        c = pv.shape[0] // 2
        out.append(jnp.where(m0, pv[:c], pv[c:]))
    return out


def _pair_queries(q, m0):
    z = jnp.zeros_like(q)
    return jnp.concatenate([jnp.where(m0, q, z), jnp.where(m0, z, q)], axis=0)


def _attn_prompt_kernel(q_ref, kc_ref, kp_ref, vc_ref, vp_ref, bias_ref, y_ref, k_scr, v_scr):
    rows = q_ref.shape[1]
    n_chunks = rows // CHUNK
    t_idx = pl.program_id(1)
    k_scr[0:WINDOW, :] = kp_ref[0]
    k_scr[WINDOW:WINDOW + rows, :] = kc_ref[0]
    v_scr[0:WINDOW, :] = vp_ref[0]
    v_scr[WINDOW:WINDOW + rows, :] = vc_ref[0]
    m0 = lax.broadcasted_iota(jnp.int32, (CHUNK, LANES), 1) < HEAD_DIM
    key_i = lax.broadcasted_iota(jnp.int32, (1, BAND), 1)
    ri = lax.broadcasted_iota(jnp.int32, (2 * CHUNK, 2 * CHUNK), 0)
    rj = lax.broadcasted_iota(jnp.int32, (2 * CHUNK, 2 * CHUNK), 1)
    eye = jnp.where(ri == rj, 1.0, 0.0).astype(BF16)

    group = math.gcd(n_chunks, ATTN_GROUP_CHUNKS)

    def group_body(gi, carry, *, masked):
        units = []
        for cc in range(group):
            r0 = pl.multiple_of((gi * group + cc) * CHUNK, CHUNK)
            band = pl.ds(r0, BAND)
            for pr in range(N_PAIRS):
                ls = slice(pr * LANES, (pr + 1) * LANES)
                lhs = jnp.concatenate([_pair_queries(q_ref[0, pl.ds(r0, CHUNK), ls], m0), eye], axis=1)
                rhs = jnp.concatenate([k_scr[band, ls], bias_ref[pr]], axis=1)
                s = _dot_nt(lhs, rhs)
                if masked:
                    s = s + jnp.where(key_i + (r0 - WINDOW) >= 0, 0.0, NEG_BIG)
                units.append([(s, v_scr[band, ls])])
        outs = _softmax_pv(units, m0)
        for cc in range(group):
            r0 = pl.multiple_of((gi * group + cc) * CHUNK, CHUNK)
            for pr in range(N_PAIRS):
                y_ref[0, pl.ds(r0, CHUNK), pr * LANES:(pr + 1) * LANES] = (
                    outs[cc * N_PAIRS + pr].astype(BF16))
        return carry

    @pl.when(t_idx == 0)
    def _():
        lax.fori_loop(0, n_chunks // group, functools.partial(group_body, masked=True), 0)

    @pl.when(t_idx > 0)
    def _():
        lax.fori_loop(0, n_chunks // group, functools.partial(group_body, masked=False), 0)


def _attn_prompt(q, k, v, bias_pairs, *, rows):
    b, t, _ = q.shape
    assert rows == WINDOW
    cur = pl.BlockSpec((1, rows, D_A), lambda i, j: (i, j, 0))
    prev = pl.BlockSpec((1, rows, D_A), lambda i, j: (i, jnp.maximum(j - 1, 0), 0))
    return pl.pallas_call(
        _attn_prompt_kernel,
        grid=(b, t // rows),
        in_specs=[cur, cur, prev, cur, prev, _const_spec((N_PAIRS, BAND, LANES))],
        out_specs=cur,
        out_shape=jax.ShapeDtypeStruct((b, t, D_A), BF16),
        scratch_shapes=[pltpu.VMEM((WINDOW + rows, D_A), BF16),
                        pltpu.VMEM((WINDOW + rows, D_A), BF16)],
        compiler_params=pltpu.CompilerParams(
            dimension_semantics=("arbitrary", "arbitrary"), vmem_limit_bytes=VMEM_LIMIT),
        name="attn_prompt",
    )(q, k, k, v, v, bias_pairs)


def _attn_sample_kernel(q_ref, k_ref, v_ref, ck_ref, cv_ref, bias_c_ref, bias_n_ref, y_ref):
    rows = q_ref.shape[1]
    m0 = lax.broadcasted_iota(jnp.int32, (rows, LANES), 1) < HEAD_DIM
    units = []
    for pr in range(N_PAIRS):
        ls = slice(pr * LANES, (pr + 1) * LANES)
        lhs = _pair_queries(q_ref[0, :, ls], m0)
        s_c = _dot_nt(lhs, ck_ref[0, :, ls].astype(BF16)) + bias_c_ref[pr]
        s_n = _dot_nt(lhs, k_ref[0, :, ls]) + bias_n_ref[pr]
        units.append([(s_c, cv_ref[0, :, ls].astype(BF16)), (s_n, v_ref[0, :, ls])])
    for pr, o in enumerate(_softmax_pv(units, m0)):
        y_ref[0, :, pr * LANES:(pr + 1) * LANES] = o.astype(BF16)


def _attn_sample(q, k, v, cache_k, cache_v, bias_c, bias_n):
    b, rows, _ = q.shape
    n_cache = cache_k.shape[1]
    cur = pl.BlockSpec((1, rows, D_A), lambda i: (i, 0, 0))
    cache = pl.BlockSpec((1, n_cache, D_A), lambda i: (i, 0, 0))
    return pl.pallas_call(
        _attn_sample_kernel,
        grid=(b,),
        in_specs=[cur, cur, cur, cache, cache,
                  _const_spec((N_PAIRS, 2 * rows, n_cache)), _const_spec((N_PAIRS, 2 * rows, rows))],
        out_specs=cur,
        out_shape=jax.ShapeDtypeStruct((b, rows, D_A), BF16),
        compiler_params=pltpu.CompilerParams(dimension_semantics=("arbitrary",)),
        name="attn_sample",
    )(q, k, v, cache_k, cache_v, bias_c, bias_n)


def _rel_bias(table, n_q, n_k, offset):
    h = table.shape[0]
    period = n_q + n_k
    e = jnp.arange(period)
    e = jnp.where(e < n_k, e, e - period)
    u = table[:, jnp.clip(offset - e, -REL_CLIP, REL_CLIP) + REL_CLIP].astype(F32) * LOG2E
    toeplitz = jnp.tile(u, (1, n_q))[:, :n_q * (period - 1)].reshape(h, n_q, period - 1)
    return toeplitz[:, :, :n_k]


def _pair_rows(bias):
    h, nq, nk = bias.shape
    return bias.reshape(h // 2, 2 * nq, nk)


def _out_kernel(x_ref, o_ref, post_ref, ya_ref, gt_ref, mod_ref, lng_ref, lnb_ref, ones_ref,
                wbr_ref, wba_ref, wo_ref, ng_ref, w1_ref, w2_ref, fg_ref, y_ref):
    nb, rows, d = x_ref.shape
    n = nb * rows
    mod = mod_ref[...]
    g1 = mod[:, :, 2 * d:3 * d]
    sh2 = mod[:, :, 3 * d:4 * d]
    sc2 = mod[:, :, 4 * d:5 * d]
    g2 = mod[:, :, 5 * d:6 * d]

    gates = gt_ref[...].reshape(n, 2 * d).astype(F32)
    branch_a = gates[:, d:2 * d] * _dot(ya_ref[...].reshape(n, D_A), wba_ref[...])

    head_ones = ones_ref[...]
    o = o_ref[...].reshape(n, D_R)
    post = post_ref[...].reshape(n, 2 * D_R).astype(F32)
    inv_n = 1.0 / HEAD_DIM
    mean = _head_sums(o, head_ones) * inv_n
    cen = o.astype(F32) - mean
    var = _head_sums(cen * cen, head_ones) * inv_n
    o_n = cen * lax.rsqrt(var + GN_EPS) * lng_ref[...] + lnb_ref[...]
    y_r = ((o_n + post[:, D_R:2 * D_R]) * post[:, 0:D_R]).astype(BF16)

    merged = gates[:, 0:d] * _dot(y_r, wbr_ref[...]) + branch_a
    mix = _dot(merged.astype(BF16), wo_ref[...]).reshape(nb, rows, d)
    x1 = x_ref[...] + g1 * mix

    h2 = ((_rms(x1) * ng_ref[...]) * (1.0 + sc2) + sh2).reshape(n, d).astype(BF16)
    ff_blk = 1024
    n_blk = D_FF // ff_blk
    acc = None
    mid = _dot(h2, w1_ref[:, 0:ff_blk])
    for j in range(n_blk):
        nxt = _dot(h2, w1_ref[:, (j + 1) * ff_blk:(j + 2) * ff_blk]) if j + 1 < n_blk else None
        act = jnp.square(jnp.maximum(mid, 0.0)).astype(BF16)
        part = _dot(act, w2_ref[j * ff_blk:(j + 1) * ff_blk, :])
        acc = part if acc is None else acc + part
        mid = nxt
    x2 = x1 + g2 * acc.reshape(nb, rows, d)
    y_ref[...] = _rms(x2) * fg_ref[...]


def _out(x, o, post, y_a, gates, mod, lnx_g, lnx_b, head_ones,
         w_br_r, w_br_a, w_out, norm_g, w_ff1, w_ff2, final_g, *, nb, rows):
    b, t, d = x.shape
    tile = lambda n: pl.BlockSpec((nb, rows, n), lambda i, j: (i, j, 0))
    return pl.pallas_call(
        _out_kernel,
        grid=(b // nb, t // rows),
        in_specs=[tile(d), tile(D_R), tile(2 * D_R), tile(D_A), tile(2 * d),
                  pl.BlockSpec((nb, 1, 6 * d), lambda i, j: (i, 0, 0)),
                  _const_spec((1, D_R)), _const_spec((1, D_R)), _const_spec((HEAD_SUM_BLOCK, HEAD_SUM_BLOCK)),
                  _const_spec((D_R, d)), _const_spec((D_A, d)), _const_spec((d, d)),
                  _const_spec((1, 1, d)),
                  _const_spec((d, D_FF)), _const_spec((D_FF, d)),
                  _const_spec((1, 1, d))],
        out_specs=tile(d),
        out_shape=jax.ShapeDtypeStruct((b, t, d), F32),
        compiler_params=pltpu.CompilerParams(
            dimension_semantics=("arbitrary", "arbitrary"), vmem_limit_bytes=VMEM_LIMIT),
        name="out",
    )(x, o, post, y_a, gates, mod, lnx_g, lnx_b, head_ones,
      w_br_r, w_br_a, w_out, norm_g.reshape(1, 1, d), w_ff1, w_ff2, final_g.reshape(1, 1, d))


def _group(x, mod, prev0, s0, attend, W, *, nb, rows, n_tail):
    b, t, d = x.shape
    wkv_in, lw, post, q, k, v, gates, p_tail, k_tail, v_tail = _proj(
        x, mod, W["norm_mix_g"], W["w_in"], prev0, W["mu_shift"], W["rwkv_in"],
        nb=nb, rows=rows, n_tail=n_tail)
    t_pad = -(-t // CHUNK) * CHUNK
    pad = lambda a: jnp.pad(a, ((0, 0), (0, t_pad - t), (0, 0)))
    o, s_pairs = _wkv(pad(wkv_in), pad(lw), _state_to_pairs(s0),
                      rows=min(t_pad, WKV_ROWS), n_seq=math.gcd(b, WKV_SEQS))
    y_a = attend(q, k, v)
    y = _out(x, o[:, :t], post, y_a, gates, mod, W["lnx_g"], W["lnx_b"], W["rwkv_in"][-1],
             W["w_br_r"], W["w_br_a"], W["w_out"], W["norm_mlp_g"],
             W["w_ff1"], W["w_ff2"], W["final_norm_g"], nb=nb, rows=rows)
    return (y, _pairs_to_state(s_pairs)[None], p_tail[None, :, 7, :],
            k_tail.reshape(1, b, n_tail, N_HEADS, HEAD_DIM),
            v_tail.reshape(1, b, n_tail, N_HEADS, HEAD_DIM))


def kernel(x_prompt, x_sample, c_prompt, c_sample, state_rwkv_wkv, state_rwkv_shift, cache_att_k, cache_att_v, w_ada, b_ada, norm_mix_g, w_in, mu_shift, w_lora_up, w0, a_lora_up, a0, g_lora_up, k_k, k_a, r_k, lnx_g, lnx_b, rel_table, w_br_r, w_br_a, w_out, norm_mlp_g, w_ff1, w_ff2, final_norm_g):
    bp, tp, d = x_prompt.shape
    bs, ts, _ = x_sample.shape
    l = 0
    row = lambda a: a.reshape(1, -1)
    zl = jnp.zeros((LORA_W, D_R), F32)
    w_wa = jnp.concatenate([jnp.concatenate([w_lora_up[l], zl], axis=1),
                            jnp.concatenate([zl, a_lora_up[l]], axis=1)], axis=0).astype(BF16)
    head_id = jnp.arange(HEAD_SUM_BLOCK) // HEAD_DIM
    head_ones = (head_id[:, None] == head_id[None, :]).astype(BF16)
    W = dict(
        norm_mix_g=norm_mix_g[l], w_in=w_in[l].astype(BF16), mu_shift=mu_shift[l],
        rwkv_in=(w_wa, row(w0[l]), row(a0[l]), g_lora_up[l].astype(BF16),
                 row(k_k[l]), row(k_a[l]), row(r_k[l]), head_ones),
        lnx_g=row(lnx_g[l]), lnx_b=row(lnx_b[l]),
        w_br_r=w_br_r[l].astype(BF16), w_br_a=w_br_a[l].astype(BF16), w_out=w_out[l].astype(BF16),
        norm_mlp_g=norm_mlp_g[l], w_ff1=w_ff1[l].astype(BF16), w_ff2=w_ff2[l].astype(BF16),
        final_norm_g=final_norm_g)

    mod = _ada(jnp.concatenate([c_prompt, c_sample], axis=0), w_ada[l], b_ada[l])
    mod_p = mod[:bp, None, :]
    mod_s = mod[bp:, None, :]

    bias_p = _rel_bias(rel_table[l], CHUNK, BAND, WINDOW)
    bias_t = bias_p.reshape(N_PAIRS, 2 * CHUNK, BAND).transpose(0, 2, 1).astype(BF16)
    attend_p = lambda q, k, v: _attn_prompt(q, k, v, bias_t, rows=WINDOW)
    n_keep = min(WINDOW, tp)
    y_p, p_wkv, p_shift, p_k, p_v = _group(
        x_prompt, mod_p, jnp.zeros((bp, 8, R_PROJ), F32),
        jnp.zeros((bp, N_HEADS, HEAD_DIM, HEAD_DIM), F32), attend_p, W,
        nb=1, rows=WINDOW, n_tail=n_keep)

    n_cache = cache_att_k.shape[2]
    bias_s = _pair_rows(_rel_bias(rel_table[l], ts, n_cache + ts, n_cache))
    ck = cache_att_k[l].reshape(bs, n_cache, D_A)
    cv = cache_att_v[l].reshape(bs, n_cache, D_A)
    attend_s = lambda q, k, v: _attn_sample(q, k, v, ck, cv, bias_s[:, :, :n_cache], bias_s[:, :, n_cache:])
    prev_s = jnp.broadcast_to(state_rwkv_shift[l][:, None, :], (bs, 8, R_PROJ))
    y_s, s_wkv, s_shift, s_k, s_v = _group(
        x_sample, mod_s, prev_s, state_rwkv_wkv[l], attend_s, W, nb=bs, rows=ts, n_tail=ts)

    return (y_p, y_s, p_wkv, p_shift, p_k, p_v, s_wkv, s_shift, s_k, s_v)
```
